```python
import math
import jax, jax.numpy as jnp
from jax import lax
import numpy as np

D_MODEL = 1024
BATCH = 8
SEQ = 2048
DEPTH = 4

MLA_HEADS = 8
MLA_NOPE = 64
MLA_ROPE = 32
MLA_V = 64
MLA_Q_RANK = 256
MLA_KV_RANK = 128
ROPE_BASE = 10000.0
Q_BLOCK = 128
POS_MAX_OFFSET = 4096

MLSTM_HEADS = 4
MLSTM_DH = 64
MLSTM_CHUNK = 64
M_INIT = -1e30

SSD_HEADS = 4
SSD_HEADDIM = 64
SSD_GROUPS = 2
SSD_HPG = SSD_HEADS // SSD_GROUPS
SSD_STATE = 128
SSD_CONV = 5
SSD_CHUNK = 128

MLA_OUT = MLA_HEADS * MLA_V
MLSTM_W = MLSTM_HEADS * MLSTM_DH
SSD_INNER = SSD_HEADS * SSD_HEADDIM
MIX_WIDTH = MLA_OUT + MLSTM_W + SSD_INNER
SSD_CONV_DIM = SSD_INNER + 2 * SSD_GROUPS * SSD_STATE
IN_SIZES = (MLA_Q_RANK, MLA_KV_RANK, MLA_ROPE,
            MLSTM_W, MLSTM_W, MLSTM_W, MLSTM_W, 4 * MLSTM_HEADS,
            SSD_INNER, SSD_CONV_DIM, 2 * SSD_HEADS)
IN_WIDTH = sum(IN_SIZES)

N_GROUPS = 4
EXPERTS_PER_GROUP = 4
N_EXPERTS = N_GROUPS * EXPERTS_PER_GROUP
TOP_K = 2
D_EXPERT = 256

DEEPNORM_ALPHA = (2 * DEPTH) ** 0.25
DEEPNORM_BETA = (8 * DEPTH) ** -0.25

kernel_name = "hymba_mla_mlstm_ssd_hmoe_deepnorm"

F32 = jnp.float32


def _split(t, sizes):
    out, off = [], 0
    for s in sizes:
        out.append(t[..., off:off + s])
        off += s
    return out


def _layernorm(x, g, b, eps=1e-5):
    xf = x.astype(F32)
    mu = jnp.mean(xf, -1, keepdims=True)
    var = jnp.mean(jnp.square(xf - mu), -1, keepdims=True)
    return (xf - mu) * lax.rsqrt(var + eps) * g + b


def _rmsnorm(x, g, eps=1e-6):
    xf = x.astype(F32)
    return xf * lax.rsqrt(jnp.mean(xf * xf, -1, keepdims=True) + eps) * g


def _rope_table(positions):
    inv = ROPE_BASE ** (-jnp.arange(0, MLA_ROPE, 2, dtype=F32) / MLA_ROPE)
    ang = positions.astype(F32)[..., None] * inv
    return jnp.cos(ang), jnp.sin(ang)


def _apply_rope(t, cos, sin):
    half = t.shape[-1] // 2
    t1, t2 = t[..., :half].astype(F32), t[..., half:].astype(F32)
    return jnp.concatenate([t1 * cos - t2 * sin, t2 * cos + t1 * sin], -1)


def _mla(c_q, c_kv, k_r, cos, sin, q_norm, kv_norm, w_uq, w_ukv):
    Bsz, S, _ = c_q.shape
    q = (_rmsnorm(c_q, q_norm).astype(c_q.dtype) @ w_uq).reshape(Bsz, S, MLA_HEADS, MLA_NOPE + MLA_ROPE)
    q_nope = q[..., :MLA_NOPE]
    q_rope = _apply_rope(q[..., MLA_NOPE:], cos[:, :, None], sin[:, :, None]).astype(q.dtype)
    kv = (_rmsnorm(c_kv, kv_norm).astype(c_kv.dtype) @ w_ukv).reshape(Bsz, S, MLA_HEADS, MLA_NOPE + MLA_V)
    k_nope, v = kv[..., :MLA_NOPE], kv[..., MLA_NOPE:]
    k_rope = _apply_rope(k_r, cos, sin).astype(k_r.dtype)
    scale = (MLA_NOPE + MLA_ROPE) ** -0.5
    nb = S // Q_BLOCK
    qn = q_nope.reshape(Bsz, nb, Q_BLOCK, MLA_HEADS, MLA_NOPE).transpose(1, 0, 2, 3, 4)
    qr = q_rope.reshape(Bsz, nb, Q_BLOCK, MLA_HEADS, MLA_ROPE).transpose(1, 0, 2, 3, 4)

    def block(args):
        qn_b, qr_b = args
        s = (jnp.einsum('bqhd,bkhd->bhqk', qn_b, k_nope)
             + jnp.einsum('bqhr,bkr->bhqk', qr_b, k_rope)).astype(F32) * scale
        p = jax.nn.softmax(s, axis=-1).astype(v.dtype)
        return jnp.einsum('bhqk,bkhd->bqhd', p, v)

    o = lax.map(block, (qn, qr))
    return o.transpose(1, 0, 2, 3, 4).reshape(Bsz, S, MLA_OUT)


def _mlstm_scan(q, k, v, ig, lf):
    Bsz, S, H, Dh = q.shape
    L = MLSTM_CHUNK
    nc = S // L

    def chunks(t):
        t = t.reshape((Bsz, nc, L, H) + t.shape[3:])
        return jnp.moveaxis(jnp.moveaxis(t, 1, 0), 3, 2)

    mask = jnp.tril(jnp.ones((L, L), bool))

    def step(carry, inp):
        C, n, m = carry
        qc, kc, vc, igc, lfc = inp
        b = jnp.cumsum(lfc, axis=-1)
        a = jnp.where(mask, b[..., :, None] - b[..., None, :] + igc[..., None, :], -jnp.inf)
        g = b + m[..., None]
        m_t = jnp.maximum(g, jnp.max(a, -1))
        w_intra = jnp.exp(a - m_t[..., None])
        w_inter = jnp.exp(g - m_t)
        qk = jnp.einsum('bhtd,bhsd->bhts', qc, kc).astype(F32) * w_intra
        num = (jnp.einsum('bhts,bhsd->bhtd', qk, vc)
               + w_inter[..., None] * jnp.einsum('bhtd,bhde->bhte', qc, C))
        den = jnp.sum(qk, -1) + w_inter * jnp.einsum('bhtd,bhd->bht', qc, n)
        h = num / jnp.maximum(jnp.abs(den), jnp.exp(-m_t))[..., None]
        bL = b[..., -1]
        dec = bL[..., None] - b + igc
        m_new = jnp.maximum(bL + m, jnp.max(dec, -1))
        w_s = jnp.exp(dec - m_new[..., None])
        w_c = jnp.exp(bL + m - m_new)
        C = w_c[..., None, None] * C + jnp.einsum('bhs,bhsd,bhse->bhde', w_s, kc, vc)
        n = w_c[..., None] * n + jnp.einsum('bhs,bhsd->bhd', w_s, kc)
        return (C, n, m_new), h

    init = (jnp.zeros((Bsz, H, Dh, Dh), F32), jnp.zeros((Bsz, H, Dh), F32),
            jnp.full((Bsz, H), M_INIT, F32))
    _, hs = lax.scan(step, init, (chunks(q), chunks(k), chunks(v), chunks(ig), chunks(lf)))
    hs = jnp.moveaxis(jnp.moveaxis(hs, 2, 3), 0, 1)
    return hs.reshape(Bsz, S, H, Dh)


def _mlstm(q, k, v, o, gates, gate_bias, norm_g):
    Bsz, S, _ = q.shape
    shp = (Bsz, S, MLSTM_HEADS, MLSTM_DH)
    q, k, v = q.reshape(shp), k.reshape(shp) * (MLSTM_DH ** -0.5), v.reshape(shp)
    gt = gates.reshape(Bsz, S, 4, MLSTM_HEADS).astype(F32) + gate_bias
    ig_f, lf_f = gt[:, :, 0], jax.nn.log_sigmoid(gt[:, :, 1])
    ig_b, lf_b = gt[:, :, 2], jax.nn.log_sigmoid(gt[:, :, 3])
    flip = lambda t: jnp.flip(t, axis=1)
    h = (_mlstm_scan(q, k, v, ig_f, lf_f)
         + flip(_mlstm_scan(flip(q), flip(k), flip(v), flip(ig_b), flip(lf_b))))
    mu = jnp.mean(h, -1, keepdims=True)
    var = jnp.mean(jnp.square(h - mu), -1, keepdims=True)
    hn = (h - mu) * lax.rsqrt(var + 1e-5) * norm_g.reshape(MLSTM_HEADS, MLSTM_DH)
    y = jax.nn.sigmoid(o.reshape(shp).astype(F32)) * hn
    return y.reshape(Bsz, S, MLSTM_W)


def _ssd_scan(x, dt, A, Bm, Cm):
    Bsz, S, G, E, P = x.shape
    N = Bm.shape[-1]
    L = SSD_CHUNK
    nc = S // L

    def chunks(t):
        return jnp.moveaxis(t.reshape((Bsz, nc, L) + t.shape[2:]), 1, 0)

    mask = jnp.tril(jnp.ones((L, L), bool))[None, :, :, None, None]

    def step(state, inp):
        xc, dtc, Bc, Cc = inp
        cs = jnp.cumsum(dtc * A, axis=1)
        Lm = jnp.exp(jnp.where(mask, cs[:, :, None] - cs[:, None, :], -jnp.inf))
        CB = jnp.einsum('btgn,bsgn->btsg', Cc, Bc)
        xdt = xc * dtc[..., None]
        y = jnp.einsum('btsg,btsge,bsgep->btgep', CB, Lm, xdt)
        y = y + jnp.einsum('btgn,bgepn->btgep', Cc, state) * jnp.exp(cs)[..., None]
        decay_end = jnp.exp(cs[:, -1:] - cs)
        state = (jnp.exp(cs[:, -1])[..., None, None] * state
                 + jnp.einsum('bsgn,bsge,bsgep->bgepn', Bc, decay_end, xdt))
        return state, y

    init = jnp.zeros((Bsz, G, E, P, N), F32)
    _, ys = lax.scan(step, init, (chunks(x), chunks(dt), chunks(Bm), chunks(Cm)))
    return jnp.moveaxis(ys, 0, 1).reshape(Bsz, S, G, E, P)


def _ssd(z, xbc, dt_raw, conv_w, conv_b, dt_bias, a_log, d_skip, norm_g):
    Bsz, S, Cdim = xbc.shape
    pad = SSD_CONV // 2
    xbc = lax.conv_general_dilated(xbc, conv_w[:, None, :], window_strides=(1,),
                                   padding=((pad, pad),), dimension_numbers=('NWC', 'WIO', 'NWC'),
                                   feature_group_count=Cdim) + conv_b
    xbc = jax.nn.silu(xbc)
    xs, Bm, Cm = _split(xbc, (SSD_INNER, SSD_GROUPS * SSD_STATE, SSD_GROUPS * SSD_STATE))
    x = xs.reshape(Bsz, S, SSD_GROUPS, SSD_HPG, SSD_HEADDIM)
    Bm = Bm.reshape(Bsz, S, SSD_GROUPS, SSD_STATE)
    Cm = Cm.reshape(Bsz, S, SSD_GROUPS, SSD_STATE)
    dt = jax.nn.softplus(dt_raw.reshape(Bsz, S, 2, SSD_GROUPS, SSD_HPG).astype(F32)
                         + dt_bias.reshape(2, SSD_GROUPS, SSD_HPG))
    A = -jnp.exp(a_log.astype(F32)).reshape(2, SSD_GROUPS, SSD_HPG)
    flip = lambda t: jnp.flip(t, axis=1)
    y = (_ssd_scan(x, dt[:, :, 0], A[0], Bm, Cm)
         + flip(_ssd_scan(flip(x), flip(dt[:, :, 1]), A[1], flip(Bm), flip(Cm)))
         + x * d_skip.reshape(SSD_GROUPS, SSD_HPG)[..., None])
    y = y.reshape(Bsz, S, SSD_INNER) * jax.nn.silu(z.astype(F32))
    y = _rmsnorm(y.reshape(Bsz, S, SSD_GROUPS, SSD_INNER // SSD_GROUPS),
                 norm_g.reshape(SSD_GROUPS, SSD_INNER // SSD_GROUPS))
    return y.reshape(Bsz, S, SSD_INNER)


def _hier_moe(x, wg, bg, we, be, w_gate, w_up, w_down):
    glog = (jnp.einsum('bsd,dg->bsg', x, wg) + bg).astype(F32)
    gp, gi = lax.top_k(jax.nn.softmax(glog, -1), 1)
    elog = (jnp.einsum('bsd,gde->bsge', x, we) + be).astype(F32)
    elog = jnp.einsum('bsge,bsg->bse', elog, jax.nn.one_hot(gi[..., 0], N_GROUPS, dtype=F32))
    ep, ei = lax.top_k(jax.nn.softmax(elog, -1), TOP_K)
    ep = ep / jnp.sum(ep, -1, keepdims=True)
    eid = gi * EXPERTS_PER_GROUP + ei
    combine = jnp.sum(jax.nn.one_hot(eid, N_EXPERTS, dtype=F32) * (gp * ep)[..., None], axis=2)
    h = jax.nn.silu(jnp.einsum('bsd,edf->bsef', x, w_gate)) * jnp.einsum('bsd,edf->bsef', x, w_up)
    h = h * combine[..., None].astype(h.dtype)
    return jnp.einsum('bsef,efd->bsd', h, w_down)


def setup_inputs(seed: int = 0) -> dict:
    key = jax.random.key(seed)
    ks = jax.random.split(key, 32)
    nrm = lambda k, shape, s: jax.random.normal(k, shape, F32) * s
    gain = lambda k, shape: 1.0 + 0.02 * jax.random.normal(k, shape, F32)

    x = jax.random.normal(ks[0], (BATCH, SEQ, D_MODEL), F32)
    positions = (jax.random.randint(ks[1], (BATCH, 1), 0, POS_MAX_OFFSET, dtype=jnp.int32)
                 + jnp.arange(SEQ, dtype=jnp.int32)[None, :])

    w_in = nrm(ks[2], (DEPTH, D_MODEL, IN_WIDTH), D_MODEL ** -0.5)
    mla_q_norm = gain(ks[3], (DEPTH, MLA_Q_RANK))
    mla_kv_norm = gain(ks[4], (DEPTH, MLA_KV_RANK))
    mla_w_uq = nrm(ks[5], (DEPTH, MLA_Q_RANK, MLA_HEADS * (MLA_NOPE + MLA_ROPE)), MLA_Q_RANK ** -0.5)
    mla_w_ukv = nrm(ks[6], (DEPTH, MLA_KV_RANK, MLA_HEADS * (MLA_NOPE + MLA_V)), MLA_KV_RANK ** -0.5)

    lin = jnp.linspace(3.0, 6.0, MLSTM_HEADS, dtype=F32)
    zer = jnp.zeros((MLSTM_HEADS,), F32)
    mlstm_gate_bias = jnp.stack([zer, lin, zer, lin])[None] + nrm(ks[7], (DEPTH, 4, MLSTM_HEADS), 0.1)
    mlstm_norm = gain(ks[8], (DEPTH, MLSTM_W))

    ssd_conv_w = nrm(ks[9], (DEPTH, SSD_CONV, SSD_CONV_DIM), SSD_CONV ** -0.5)
    ssd_conv_b = nrm(ks[10], (DEPTH, SSD_CONV_DIM), 0.01)
    dt0 = jnp.exp(jax.random.uniform(ks[11], (DEPTH, 2, SSD_HEADS), F32, math.log(1e-3), math.log(1e-1)))
    ssd_dt_bias = dt0 + jnp.log(-jnp.expm1(-dt0))
    ssd_a_log = jnp.log(jax.random.uniform(ks[12], (DEPTH, 2, SSD_HEADS), F32, 1.0, 16.0))
    ssd_d = gain(ks[13], (DEPTH, SSD_HEADS))
    ssd_norm = gain(ks[14], (DEPTH, SSD_INNER))

    w_out = nrm(ks[15], (DEPTH, MIX_WIDTH, D_MODEL), MIX_WIDTH ** -0.5 * DEEPNORM_BETA)
    ln1_g = gain(ks[16], (DEPTH, D_MODEL))
    ln1_b = nrm(ks[17], (DEPTH, D_MODEL), 0.02)

    router_group_w = nrm(ks[18], (DEPTH, D_MODEL, N_GROUPS), D_MODEL ** -0.5)
    router_group_b = nrm(ks[19], (DEPTH, N_GROUPS), 0.01)
    router_expert_w = nrm(ks[20], (DEPTH, N_GROUPS, D_MODEL, EXPERTS_PER_GROUP), D_MODEL ** -0.5)
    router_expert_b = nrm(ks[21], (DEPTH, N_GROUPS, EXPERTS_PER_GROUP), 0.01)
    expert_w_gate = nrm(ks[22], (DEPTH, N_EXPERTS, D_MODEL, D_EXPERT), D_MODEL ** -0.5)
    expert_w_up = nrm(ks[23], (DEPTH, N_EXPERTS, D_MODEL, D_EXPERT), D_MODEL ** -0.5)
    expert_w_down = nrm(ks[24], (DEPTH, N_EXPERTS, D_EXPERT, D_MODEL), D_EXPERT ** -0.5 * DEEPNORM_BETA)
    ln2_g = gain(ks[25], (DEPTH, D_MODEL))
    ln2_b = nrm(ks[26], (DEPTH, D_MODEL), 0.02)

    return {"x": x, "positions": positions, "w_in": w_in,
            "mla_q_norm": mla_q_norm, "mla_kv_norm": mla_kv_norm,
            "mla_w_uq": mla_w_uq, "mla_w_ukv": mla_w_ukv,
            "mlstm_gate_bias": mlstm_gate_bias, "mlstm_norm": mlstm_norm,
            "ssd_conv_w": ssd_conv_w, "ssd_conv_b": ssd_conv_b, "ssd_dt_bias": ssd_dt_bias,
            "ssd_a_log": ssd_a_log, "ssd_d": ssd_d, "ssd_norm": ssd_norm,
            "w_out": w_out, "ln1_g": ln1_g, "ln1_b": ln1_b,
            "router_group_w": router_group_w, "router_group_b": router_group_b,
            "router_expert_w": router_expert_w, "router_expert_b": router_expert_b,
            "expert_w_gate": expert_w_gate, "expert_w_up": expert_w_up, "expert_w_down": expert_w_down,
            "ln2_g": ln2_g, "ln2_b": ln2_b}


def reference(x, positions, w_in, mla_q_norm, mla_kv_norm, mla_w_uq, mla_w_ukv,
              mlstm_gate_bias, mlstm_norm, ssd_conv_w, ssd_conv_b, ssd_dt_bias,
              ssd_a_log, ssd_d, ssd_norm, w_out, ln1_g, ln1_b,
              router_group_w, router_group_b, router_expert_w, router_expert_b,
              expert_w_gate, expert_w_up, expert_w_down, ln2_g, ln2_b):
    dtype = x.dtype
    cos, sin = _rope_table(positions)
    for l in range(DEPTH):
        p = jnp.einsum('bsd,dc->bsc', x, w_in[l])
        (c_q, c_kv, k_r, m_q, m_k, m_v, m_o, m_g, s_z, s_xbc, s_dt) = _split(p, IN_SIZES)
        y_a = _mla(c_q, c_kv, k_r, cos, sin, mla_q_norm[l], mla_kv_norm[l], mla_w_uq[l], mla_w_ukv[l])
        y_b = _mlstm(m_q, m_k, m_v, m_o, m_g, mlstm_gate_bias[l], mlstm_norm[l])
        y_c = _ssd(s_z, s_xbc, s_dt, ssd_conv_w[l], ssd_conv_b[l], ssd_dt_bias[l],
                   ssd_a_log[l], ssd_d[l], ssd_norm[l])
        heads = jnp.concatenate([y_a.astype(dtype), y_b.astype(dtype), y_c.astype(dtype)], -1)
        mix = jnp.einsum('bsc,cd->bsd', heads, w_out[l])
        x = _layernorm(DEEPNORM_ALPHA * x + mix, ln1_g[l], ln1_b[l]).astype(dtype)
        ffn = _hier_moe(x, router_group_w[l], router_group_b[l], router_expert_w[l], router_expert_b[l],
                        expert_w_gate[l], expert_w_up[l], expert_w_down[l]).astype(dtype)
        x = _layernorm(DEEPNORM_ALPHA * x + ffn, ln2_g[l], ln2_b[l]).astype(dtype)
    return x
```

```python
import functools
import math

import jax
import jax.numpy as jnp
from jax import lax
from jax.experimental import pallas as pl
from jax.experimental.pallas import tpu as pltpu

F32 = jnp.float32
BF16 = jnp.bfloat16
HIGHEST = lax.Precision.HIGHEST

LANES = 128
VMEM_LIMIT = 56 * 1024 * 1024

MLA_HEADS = 8
MLA_NOPE = 64
MLA_ROPE = 32
MLA_V = 64
MLA_Q_RANK = 256
MLA_KV_RANK = 128
ROPE_BASE = 10000.0
MLSTM_HEADS = 4
MLSTM_DH = 64
SSD_HEADS = 4
SSD_HEADDIM = 64
SSD_GROUPS = 2
SSD_STATE = 128
SSD_CONV = 5
N_GROUPS = 4
EXPERTS_PER_GROUP = 4
N_EXPERTS = 16
M_INIT = -1e30

MLA_OUT = MLA_HEADS * MLA_V
MLSTM_W = MLSTM_HEADS * MLSTM_DH
SSD_INNER = SSD_HEADS * SSD_HEADDIM
SSD_CONV_DIM = SSD_INNER + 2 * SSD_GROUPS * SSD_STATE
CHUNK = 128

P_MLA_W = 512
P_ML_W = 4 * MLSTM_W + LANES
P_SSD_W = SSD_INNER + SSD_CONV_DIM + LANES


def _cparams(*sem):
    return pltpu.CompilerParams(dimension_semantics=sem, vmem_limit_bytes=VMEM_LIMIT)


def _dot(a, b):
    return jnp.dot(a, b, preferred_element_type=F32)


def _dot_nt(a, b):
    return lax.dot_general(a, b, (((1,), (1,)), ((), ())), preferred_element_type=F32)


def _dot_tn(a, b):
    return lax.dot_general(a, b, (((0,), (0,)), ((), ())), preferred_element_type=F32)


def _dot_f32(a, b):
    return jnp.dot(a, b, preferred_element_type=F32, precision=HIGHEST)


def _softplus(x):
    return jnp.maximum(x, 0.0) + jnp.log1p(jnp.exp(-jnp.abs(x)))


def _sigmoid(x):
    return 1.0 / (1.0 + jnp.exp(-x))


def _silu(x):
    return x * _sigmoid(x)


def _inproj_kernel(x_ref, wa_ref, wb_ref, wc_ref, pa_ref, pb_ref, pc_ref):
    x = x_ref[...].astype(BF16)
    pa_ref[...] = _dot(x, wa_ref[...])
    pb_ref[...] = _dot(x, wb_ref[...])
    pc_ref[...] = _dot(x, wc_ref[...])


def _inproj(x2, wa, wb, wc, tm=512):
    T, D = x2.shape
    row = lambda w: pl.BlockSpec((tm, w), lambda i: (i, 0))
    full = lambda a: pl.BlockSpec(a.shape, lambda i: (0, 0))
    return pl.pallas_call(
        _inproj_kernel,
        grid=(T // tm,),
        in_specs=[row(D), full(wa), full(wb), full(wc)],
        out_specs=[row(P_MLA_W), row(P_ML_W), row(P_SSD_W)],
        out_shape=[jax.ShapeDtypeStruct((T, P_MLA_W), F32),
                   jax.ShapeDtypeStruct((T, P_ML_W), F32),
                   jax.ShapeDtypeStruct((T, P_SSD_W), F32)],
        compiler_params=_cparams("parallel"),
        name="inproj",
    )(x2, wa, wb, wc)


def _mla_prep_kernel(p_ref, cos_ref, sin_ref, qn_ref, kvn_ref, wq_ref, wk_ref, wv_ref,
                     q_ref, k_ref, v_ref):
    p = p_ref[...]
    cq = p[:, 0:MLA_Q_RANK]
    ckv = p[:, MLA_Q_RANK:MLA_Q_RANK + MLA_KV_RANK]
    krb = p[:, MLA_Q_RANK + MLA_KV_RANK:P_MLA_W]
    cqn = (cq * lax.rsqrt(jnp.mean(cq * cq, -1, keepdims=True) + 1e-6) * qn_ref[...]).astype(BF16)
    ckn = (ckv * lax.rsqrt(jnp.mean(ckv * ckv, -1, keepdims=True) + 1e-6) * kvn_ref[...]).astype(BF16)
    q = _dot(cqn, wq_ref[...])
    k = _dot(ckn, wk_ref[...])
    v_ref[...] = _dot(ckn, wv_ref[...]).astype(v_ref.dtype)
    cos = cos_ref[...]
    sin = sin_ref[...]
    lane = lax.broadcasted_iota(jnp.int32, cos.shape, 1)
    first_half = lane < MLA_NOPE + MLA_ROPE // 2

    def rope(t):
        partner = jnp.where(first_half, pltpu.roll(t, LANES - MLA_ROPE // 2, 1),
                            pltpu.roll(t, MLA_ROPE // 2, 1))
        return t * cos + partner * sin

    scale = (MLA_NOPE + MLA_ROPE) ** -0.5
    kr = rope(krb)
    for h in range(MLA_HEADS):
        sl = slice(h * LANES, (h + 1) * LANES)
        q_ref[:, sl] = (rope(q[:, sl]) * scale).astype(q_ref.dtype)
        k_ref[:, sl] = (k[:, sl] + kr).astype(k_ref.dtype)


def _mla_prep(p_mla, cos, sin, qn, kvn, wq, wk, wv, tm=512):
    T = p_mla.shape[0]
    row = lambda w: pl.BlockSpec((tm, w), lambda i: (i, 0))
    full = lambda a: pl.BlockSpec(a.shape, lambda i: (0, 0))
    HW = MLA_HEADS * LANES
    return pl.pallas_call(
        _mla_prep_kernel,
        grid=(T // tm,),
        in_specs=[row(P_MLA_W), row(LANES), row(LANES), full(qn), full(kvn), full(wq), full(wk), full(wv)],
        out_specs=[row(HW), row(HW), row(MLA_OUT)],
        out_shape=[jax.ShapeDtypeStruct((T, HW), BF16), jax.ShapeDtypeStruct((T, HW), BF16),
                   jax.ShapeDtypeStruct((T, MLA_OUT), BF16)],
        compiler_params=_cparams("parallel"),
        name="mla_prep",
    )(p_mla, cos, sin, qn, kvn, wq, wk, wv)


def _attn_kernel(q_ref, k_ref, v_ref, o_ref):
    v = v_ref[0]
    outs = []
    for hh in range(2):
        sl = slice(hh * LANES, (hh + 1) * LANES)
        s = _dot_nt(q_ref[0, :, sl], k_ref[0, :, sl])
        m = jnp.max(s, -1, keepdims=True)
        p = jnp.exp(s - m)
        l = jnp.sum(p, -1, keepdims=True)
        outs.append(_dot(p.astype(BF16), v) / l)
    lane = lax.broadcasted_iota(jnp.int32, outs[0].shape, 1)
    o_ref[0] = jnp.where(lane < MLA_V, outs[0], outs[1]).astype(o_ref.dtype)


def _attention(q, k, v, tq=256):
    B, S, _ = q.shape
    return pl.pallas_call(
        _attn_kernel,
        grid=(B, MLA_HEADS // 2, S // tq),
        in_specs=[pl.BlockSpec((1, tq, 2 * LANES), lambda b, j, i: (b, i, j)),
                  pl.BlockSpec((1, S, 2 * LANES), lambda b, j, i: (b, 0, j)),
                  pl.BlockSpec((1, S, LANES), lambda b, j, i: (b, 0, j))],
        out_specs=pl.BlockSpec((1, tq, LANES), lambda b, j, i: (b, i, j)),
        out_shape=jax.ShapeDtypeStruct((B, S, MLA_OUT), BF16),
        compiler_params=_cparams("parallel", "parallel", "arbitrary"),
        name="mla_attn",
    )(q, k, v)


def _mlstm_kernel(p_ref, gb_ref, ng_ref, y_ref, hf_ref, hb_ref, st_ref, m_ref):
    S = p_ref.shape[1]
    L = CHUNK
    nc = S // L
    NCH = 2 * MLSTM_HEADS
    st_ref[...] = jnp.zeros(st_ref.shape, F32)
    m_ref[...] = jnp.full(m_ref.shape, M_INIT, F32)

    ti = lax.broadcasted_iota(jnp.int32, (L, L), 0)
    si = lax.broadcasted_iota(jnp.int32, (L, L), 1)
    tri = (si <= ti, si >= ti)
    tri_f = tuple(t.astype(F32) for t in tri)
    lane = lax.broadcasted_iota(jnp.int32, (L, LANES), 1)
    lo = lane < MLSTM_DH
    ones_even = (lane == MLSTM_DH).astype(F32)
    ones_odd = (lane == 0).astype(F32)
    gbias = gb_ref[...]
    QO, KO, VO, GO = 0, MLSTM_W, 2 * MLSTM_W, 4 * MLSTM_W

    def chunk_step(c, carry):
        for d in range(2):
            r0 = pl.multiple_of((c if d == 0 else nc - 1 - c) * L, L)
            rows = pl.ds(r0, L)
            G = p_ref[0, rows, GO:GO + LANES] + gbias
            LF = -_softplus(-G)
            Bc = _dot_f32(tri_f[d], LF)
            Br = Bc.T
            Gr = G.T
            for j in range(MLSTM_HEADS // 2):
                pair = slice(j * LANES, (j + 1) * LANES)
                qp = p_ref[0, rows, QO + j * LANES:QO + (j + 1) * LANES]
                kp = p_ref[0, rows, KO + j * LANES:KO + (j + 1) * LANES] * (MLSTM_DH ** -0.5)
                vp = p_ref[0, rows, VO + j * LANES:VO + (j + 1) * LANES]
                kb = kp.astype(BF16)
                hs = []
                for e in range(2):
                    h = 2 * j + e
                    ch = d * MLSTM_HEADS + h
                    icol, fcol = 8 * d + h, 8 * d + 4 + h
                    sel = lo if e == 0 else ~lo
                    qm = jnp.where(sel, qp, 0.0).astype(BF16)
                    km = jnp.where(sel, kp, 0.0).astype(BF16)
                    va = jnp.where(sel, vp, ones_even if e == 0 else ones_odd)
                    bc = Bc[:, fcol:fcol + 1]
                    br = Br[fcol:fcol + 1, :]
                    ic = G[:, icol:icol + 1]
                    ir = Gr[icol:icol + 1, :]
                    m_prev = m_ref[ch][:, 0:1]
                    a = jnp.where(tri[d], bc - br + ir, -jnp.inf)
                    g = bc + m_prev
                    m_t = jnp.maximum(g, jnp.max(a, -1, keepdims=True))
                    w_intra = jnp.exp(a - m_t)
                    w_inter = jnp.exp(g - m_t)
                    qk = _dot_nt(qm, kb) * w_intra
                    st = st_ref[ch]
                    num = _dot(qk.astype(BF16), va.astype(BF16)) + w_inter * _dot(qm, st.astype(BF16))
                    dcol = MLSTM_DH if e == 0 else 0
                    den = num[:, dcol:dcol + 1]
                    hs.append(num / jnp.maximum(jnp.abs(den), jnp.exp(-m_t)))
                    bL = br[:, L - 1:L] if d == 0 else br[:, 0:1]
                    dec = bL - bc + ic
                    m_new = jnp.maximum(bL + m_prev, jnp.max(dec, 0, keepdims=True))
                    w_s = jnp.exp(dec - m_new)
                    w_c = jnp.exp(bL + m_prev - m_new)
                    st_ref[ch] = w_c * st + _dot_tn(km, (w_s * va).astype(BF16))
                    m_ref[ch] = jnp.broadcast_to(m_new, (1, LANES))
                hpair = jnp.where(lo, hs[0], hs[1])
                if d == 0:
                    hf_ref[rows, pair] = hpair
                else:
                    hb_ref[rows, pair] = hpair
        return carry

    lax.fori_loop(0, nc, chunk_step, 0)

    R = 256
    ng = ng_ref[...]
    lane_r = lax.broadcasted_iota(jnp.int32, (R, LANES), 1)
    lo_r = lane_r < MLSTM_DH

    def epilogue(r, carry):
        rows = pl.ds(pl.multiple_of(r * R, R), R)
        for j in range(MLSTM_HEADS // 2):
            pair = slice(j * LANES, (j + 1) * LANES)
            h = hf_ref[rows, pair] + hb_ref[rows, pair]
            s0 = jnp.sum(jnp.where(lo_r, h, 0.0), -1, keepdims=True)
            s1 = jnp.sum(jnp.where(lo_r, 0.0, h), -1, keepdims=True)
            dlt = h - jnp.where(lo_r, s0, s1) * (1.0 / MLSTM_DH)
            d2 = dlt * dlt
            v0 = jnp.sum(jnp.where(lo_r, d2, 0.0), -1, keepdims=True)
            v1 = jnp.sum(jnp.where(lo_r, 0.0, d2), -1, keepdims=True)
            var = jnp.where(lo_r, v0, v1) * (1.0 / MLSTM_DH)
            hn = dlt * lax.rsqrt(var + 1e-5) * ng[:, pair]
            o = p_ref[0, rows, 3 * MLSTM_W + j * LANES:3 * MLSTM_W + (j + 1) * LANES]
            y_ref[0, rows, pair] = (_sigmoid(o) * hn).astype(y_ref.dtype)
        return carry

    lax.fori_loop(0, S // R, epilogue, 0)


def _mlstm(p_ml, gate_bias, norm_g):
    B, S, _ = p_ml.shape
    NCH = 2 * MLSTM_HEADS
    return pl.pallas_call(
        _mlstm_kernel,
        grid=(B,),
        in_specs=[pl.BlockSpec((1, S, P_ML_W), lambda b: (b, 0, 0)),
                  pl.BlockSpec((1, LANES), lambda b: (0, 0)),
                  pl.BlockSpec((1, MLSTM_W), lambda b: (0, 0))],
        out_specs=pl.BlockSpec((1, S, MLSTM_W), lambda b: (b, 0, 0)),
        out_shape=jax.ShapeDtypeStruct((B, S, MLSTM_W), BF16),
        scratch_shapes=[pltpu.VMEM((S, MLSTM_W), F32), pltpu.VMEM((S, MLSTM_W), F32),
                        pltpu.VMEM((NCH, LANES, LANES), F32), pltpu.VMEM((NCH, 1, LANES), F32)],
        compiler_params=_cparams("parallel"),
        name="mlstm",
    )(p_ml, gate_bias, norm_g)


def _ssd_kernel(p_ref, cw_ref, cb_ref, dtb_ref, alog_ref, dsk_ref, ng_ref, y_ref,
                xpad_ref, xc_ref, y1_ref, y2_ref, st_ref):
    S = p_ref.shape[1]
    L = CHUNK
    nc = S // L
    PAD = 8
    ZO, XO, DO = 0, SSD_INNER, SSD_INNER + SSD_CONV_DIM

    zpad = jnp.zeros((PAD, SSD_CONV_DIM), F32)
    xpad_ref[0:PAD, :] = zpad
    xpad_ref[PAD + S:PAD + S + PAD, :] = zpad
    R = 256
    for r in range(S // R):
        xpad_ref[PAD + r * R:PAD + (r + 1) * R, :] = p_ref[0, r * R:(r + 1) * R, XO:XO + SSD_CONV_DIM]
    half = SSD_CONV // 2
    for r in range(S // R):
        acc = jnp.zeros((R, SSD_CONV_DIM), F32) + cb_ref[...]
        for kk in range(SSD_CONV):
            off = PAD + r * R + kk - half
            acc = acc + xpad_ref[off:off + R, :] * cw_ref[kk:kk + 1, :]
        xc_ref[r * R:(r + 1) * R, :] = _silu(acc)

    st_ref[...] = jnp.zeros(st_ref.shape, F32)
    ti = lax.broadcasted_iota(jnp.int32, (L, L), 0)
    si = lax.broadcasted_iota(jnp.int32, (L, L), 1)
    tri = (si <= ti, si >= ti)
    tri_f = tuple(t.astype(F32) for t in tri)
    lane = lax.broadcasted_iota(jnp.int32, (L, LANES), 1)
    lo = lane < SSD_HEADDIM
    a_neg = -jnp.exp(alog_ref[...])
    dtb = dtb_ref[...]
    BO, CO = SSD_INNER, SSD_INNER + SSD_GROUPS * SSD_STATE

    def dt_cols(rows):
        dt = _softplus(p_ref[0, rows, DO:DO + LANES] + dtb)
        return dt, dt * a_neg

    def chunk_step(c, carry):
        rf = pl.ds(pl.multiple_of(c * L, L), L)
        rb = pl.ds(pl.multiple_of((nc - 1 - c) * L, L), L)
        dt, dta = dt_cols(rf)
        cs = (_dot_f32(tri_f[0], dta), _dot_f32(tri_f[1], dta))
        csr = (cs[0].T, cs[1].T)
        dtr = dt.T
        for g in range(SSD_GROUPS):
            pair = slice(g * LANES, (g + 1) * LANES)
            xp = xc_ref[rf, pair]
            Bg = xc_ref[rf, BO + g * LANES:BO + (g + 1) * LANES].astype(BF16)
            Cg = xc_ref[rf, CO + g * LANES:CO + (g + 1) * LANES].astype(BF16)
            CB = _dot_nt(Cg, Bg)
            xb = xp.astype(BF16)
            ys = []
            for e in range(2):
                h = SSD_GROUPS * g + e
                W = jnp.zeros((L, L), F32)
                for d in range(2):
                    col = d * SSD_HEADS + h
                    arg = jnp.where(tri[d], cs[d][:, col:col + 1] - csr[d][col:col + 1, :], -jnp.inf)
                    W = W + jnp.exp(arg) * dtr[col:col + 1, :]
                ys.append(_dot((CB * W).astype(BF16), xb))
            y1_ref[rf, pair] = jnp.where(lo, ys[0], ys[1])
        for d in range(2):
            rows = rf if d == 0 else rb
            if d == 0:
                dt_d, cs_d = dt, cs[0]
            else:
                dt_d, dta_d = dt_cols(rows)
                cs_d = _dot_f32(tri_f[1], dta_d)
            for g in range(SSD_GROUPS):
                pair = slice(g * LANES, (g + 1) * LANES)
                ce = d * SSD_HEADS + SSD_GROUPS * g
                xp = xc_ref[rows, pair]
                Bg = xc_ref[rows, BO + g * LANES:BO + (g + 1) * LANES].astype(BF16)
                Cg = xc_ref[rows, CO + g * LANES:CO + (g + 1) * LANES].astype(BF16)
                st = st_ref[d * SSD_GROUPS + g]
                cs2 = jnp.where(lo, cs_d[:, ce:ce + 1], cs_d[:, ce + 1:ce + 2])
                dt2 = jnp.where(lo, dt_d[:, ce:ce + 1], dt_d[:, ce + 1:ce + 2])
                yi = _dot(Cg, st.astype(BF16)) * jnp.exp(cs2)
                if d == 0:
                    y1_ref[rows, pair] += yi
                    tot = cs2[L - 1:L, :]
                else:
                    y2_ref[rows, pair] = yi
                    tot = cs2[0:1, :]
                xdt = xp * (dt2 * jnp.exp(tot - cs2))
                st_ref[d * SSD_GROUPS + g] = jnp.exp(tot) * st + _dot_tn(Bg, xdt.astype(BF16))
        return carry

    lax.fori_loop(0, nc, chunk_step, 0)

    dsk = dsk_ref[...]
    ng = ng_ref[...]

    def epilogue(r, carry):
        rows = pl.ds(pl.multiple_of(r * R, R), R)
        for g in range(SSD_GROUPS):
            pair = slice(g * LANES, (g + 1) * LANES)
            y = y1_ref[rows, pair] + y2_ref[rows, pair] + xc_ref[rows, pair] * dsk[:, pair]
            y = y * _silu(p_ref[0, rows, ZO + g * LANES:ZO + (g + 1) * LANES])
            y = y * lax.rsqrt(jnp.mean(y * y, -1, keepdims=True) + 1e-6) * ng[:, pair]
            y_ref[0, rows, pair] = y.astype(y_ref.dtype)
        return carry

    lax.fori_loop(0, S // R, epilogue, 0)


def _ssd(p_ssd, conv_w, conv_b, dt_bias, a_log, d_skip, norm_g):
    B, S, _ = p_ssd.shape
    full = lambda a: pl.BlockSpec(a.shape, lambda b: (0, 0))
    return pl.pallas_call(
        _ssd_kernel,
        grid=(B,),
        in_specs=[pl.BlockSpec((1, S, P_SSD_W), lambda b: (b, 0, 0)),
                  full(conv_w), full(conv_b), full(dt_bias), full(a_log), full(d_skip), full(norm_g)],
        out_specs=pl.BlockSpec((1, S, SSD_INNER), lambda b: (b, 0, 0)),
        out_shape=jax.ShapeDtypeStruct((B, S, SSD_INNER), BF16),
        scratch_shapes=[pltpu.VMEM((S + 16, SSD_CONV_DIM), F32), pltpu.VMEM((S, SSD_CONV_DIM), F32),
                        pltpu.VMEM((S, SSD_INNER), F32), pltpu.VMEM((S, SSD_INNER), F32),
                        pltpu.VMEM((2 * SSD_GROUPS, SSD_STATE, LANES), F32)],
        compiler_params=_cparams("parallel"),
        name="ssd",
    )(p_ssd, conv_w, conv_b, dt_bias, a_log, d_skip, norm_g)


def _layernorm(u, g, b):
    mu = jnp.mean(u, -1, keepdims=True)
    d = u - mu
    var = jnp.mean(d * d, -1, keepdims=True)
    return d * lax.rsqrt(var + 1e-5) * g + b


def _outproj_kernel(alpha, x_ref, ya_ref, yb_ref, yc_ref, wo_ref, g_ref, b_ref, wr_ref, br_ref,
                    x1_ref, cmb_ref):
    mix = (_dot(ya_ref[...], wo_ref[0:MLA_OUT, :])
           + _dot(yb_ref[...], wo_ref[MLA_OUT:MLA_OUT + MLSTM_W, :])
           + _dot(yc_ref[...], wo_ref[MLA_OUT + MLSTM_W:, :]))
    x1 = _layernorm(alpha * x_ref[...] + mix, g_ref[...], b_ref[...])
    x1_ref[...] = x1
    logits = _dot_f32(x1, wr_ref[...]) + br_ref[...]
    lane_i = lax.broadcasted_iota(jnp.int32, logits.shape, 1)
    lane = lane_i.astype(F32)
    group_of_lane = jnp.right_shift(lane_i - N_GROUPS, 2).astype(F32)
    neg = -jnp.inf
    big = 1e6
    glm = jnp.where(lane_i < N_GROUPS, logits, neg)
    gmax = jnp.max(glm, -1, keepdims=True)
    gp = 1.0 / jnp.sum(jnp.exp(glm - gmax), -1, keepdims=True)
    gi = jnp.min(jnp.where(glm == gmax, lane, big), -1, keepdims=True)
    in_group = (lane_i >= N_GROUPS) & (lane_i < N_GROUPS + N_EXPERTS) & (group_of_lane == gi)
    elm = jnp.where(in_group, logits, neg)
    e1 = jnp.max(elm, -1, keepdims=True)
    i1 = jnp.min(jnp.where(elm == e1, lane, big), -1, keepdims=True)
    elm2 = jnp.where(lane == i1, neg, elm)
    e2 = jnp.max(elm2, -1, keepdims=True)
    i2 = jnp.min(jnp.where(elm2 == e2, lane, big), -1, keepdims=True)
    r = jnp.exp(e2 - e1)
    p1 = 1.0 / (1.0 + r)
    p2 = r / (1.0 + r)
    cmb_ref[...] = jnp.where(lane == i1, gp * p1, jnp.where(lane == i2, gp * p2, 0.0))


def _outproj(alpha, x2, ya, yb, yc, wo, g, b, wr, br, tm=512):
    T, D = x2.shape
    row = lambda w: pl.BlockSpec((tm, w), lambda i: (i, 0))
    full = lambda a: pl.BlockSpec(a.shape, lambda i: (0, 0))
    return pl.pallas_call(
        functools.partial(_outproj_kernel, alpha),
        grid=(T // tm,),
        in_specs=[row(D), row(MLA_OUT), row(MLSTM_W), row(SSD_INNER), full(wo), full(g), full(b),
                  full(wr), full(br)],
        out_specs=[row(D), row(LANES)],
        out_shape=[jax.ShapeDtypeStruct((T, D), F32), jax.ShapeDtypeStruct((T, LANES), F32)],
        compiler_params=_cparams("parallel"),
        name="outproj_ln_router",
    )(x2, ya, yb, yc, wo, g, b, wr, br)


def _moe_kernel(alpha, x_ref, cmb_ref, wg_ref, wu_ref, wd_ref, g_ref, b_ref, o_ref, xb_ref, acc_ref):
    e = pl.program_id(1)

    @pl.when(e == 0)
    def _():
        xb_ref[...] = x_ref[...].astype(BF16)
        acc_ref[...] = jnp.zeros(acc_ref.shape, F32)

    xb = xb_ref[...]
    cmb = cmb_ref[...]
    lane = lax.broadcasted_iota(jnp.int32, cmb.shape, 1)
    c = jnp.sum(jnp.where(lane == N_GROUPS + e, cmb, 0.0), -1, keepdims=True)
    h = _silu(_dot(xb, wg_ref[0])) * _dot(xb, wu_ref[0]) * c
    acc_ref[...] += _dot(h.astype(BF16), wd_ref[0])

    @pl.when(e == N_EXPERTS - 1)
    def _():
        o_ref[...] = _layernorm(alpha * x_ref[...] + acc_ref[...], g_ref[...], b_ref[...])


def _moe(alpha, x1, cmb, wg, wu, wd, g, b, tm=1024):
    T, D = x1.shape
    F = wg.shape[-1]
    return pl.pallas_call(
        functools.partial(_moe_kernel, alpha),
        grid=(T // tm, N_EXPERTS),
        in_specs=[pl.BlockSpec((tm, D), lambda i, e: (i, 0)),
                  pl.BlockSpec((tm, LANES), lambda i, e: (i, 0)),
                  pl.BlockSpec((1, D, F), lambda i, e: (e, 0, 0)),
                  pl.BlockSpec((1, D, F), lambda i, e: (e, 0, 0)),
                  pl.BlockSpec((1, F, D), lambda i, e: (e, 0, 0)),
                  pl.BlockSpec((1, D), lambda i, e: (0, 0)),
                  pl.BlockSpec((1, D), lambda i, e: (0, 0))],
        out_specs=pl.BlockSpec((tm, D), lambda i, e: (i, 0)),
        out_shape=jax.ShapeDtypeStruct((T, D), F32),
        scratch_shapes=[pltpu.VMEM((tm, D), BF16), pltpu.VMEM((tm, D), F32)],
        compiler_params=_cparams("parallel", "arbitrary"),
        name="moe_ln",
    )(x1, cmb, wg, wu, wd, g, b)


def _pad_lanes(a, width=LANES):
    return jnp.pad(a, [(0, 0)] * (a.ndim - 1) + [(0, width - a.shape[-1])])


def kernel(x, positions, w_in, mla_q_norm, mla_kv_norm, mla_w_uq, mla_w_ukv, mlstm_gate_bias, mlstm_norm,
           ssd_conv_w, ssd_conv_b, ssd_dt_bias, ssd_a_log, ssd_d, ssd_norm, w_out, ln1_g, ln1_b,
           router_group_w, router_group_b, router_expert_w, router_expert_b,
           expert_w_gate, expert_w_up, expert_w_down, ln2_g, ln2_b):
    B, S, D = x.shape
    depth = w_in.shape[0]
    T = B * S
    alpha = (2 * depth) ** 0.25

    inv = ROPE_BASE ** (-jnp.arange(0, MLA_ROPE, 2, dtype=F32) / MLA_ROPE)
    ang = positions.astype(F32).reshape(T, 1) * inv
    cos, sin = jnp.cos(ang), jnp.sin(ang)
    ones = jnp.ones((T, MLA_NOPE), F32)
    zeros = jnp.zeros((T, MLA_NOPE), F32)
    pad = jnp.zeros((T, LANES - MLA_NOPE - MLA_ROPE), F32)
    cos_t = jnp.concatenate([ones, cos, cos, pad], -1)
    sin_t = jnp.concatenate([zeros, -sin, sin, pad], -1)

    o = [0]
    for s in (MLA_Q_RANK, MLA_KV_RANK, MLA_ROPE, MLSTM_W, MLSTM_W, MLSTM_W, MLSTM_W, 4 * MLSTM_HEADS,
              SSD_INNER, SSD_CONV_DIM, 2 * SSD_HEADS):
        o.append(o[-1] + s)
    zc = lambda n: jnp.zeros((depth, D, n), w_in.dtype)
    w_a = jnp.concatenate([w_in[..., o[0]:o[2]], zc(MLA_NOPE), w_in[..., o[2]:o[3]],
                           zc(LANES - MLA_NOPE - MLA_ROPE)], -1).astype(BF16)
    w_b = jnp.concatenate([w_in[..., o[3]:o[8]], zc(LANES - 4 * MLSTM_HEADS)], -1).astype(BF16)
    w_c = jnp.concatenate([w_in[..., o[8]:o[11]], zc(LANES - 2 * SSD_HEADS)], -1).astype(BF16)

    wq = mla_w_uq.reshape(depth, MLA_Q_RANK, MLA_HEADS, MLA_NOPE + MLA_ROPE)
    wq = _pad_lanes(wq).reshape(depth, MLA_Q_RANK, MLA_HEADS * LANES).astype(BF16)
    wkv = mla_w_ukv.reshape(depth, MLA_KV_RANK, MLA_HEADS, MLA_NOPE + MLA_V)
    wk = _pad_lanes(wkv[..., :MLA_NOPE]).reshape(depth, MLA_KV_RANK, MLA_HEADS * LANES).astype(BF16)
    wv = wkv[..., MLA_NOPE:].reshape(depth, MLA_KV_RANK, MLA_OUT).astype(BF16)

    gate_bias = _pad_lanes(mlstm_gate_bias.reshape(depth, 1, 4 * MLSTM_HEADS))
    dt_bias = _pad_lanes(ssd_dt_bias.reshape(depth, 1, 2 * SSD_HEADS))
    a_log = _pad_lanes(ssd_a_log.reshape(depth, 1, 2 * SSD_HEADS))
    d_skip = jnp.repeat(ssd_d, SSD_HEADDIM, axis=-1).reshape(depth, 1, SSD_INNER)
    conv_w = jnp.pad(ssd_conv_w, ((0, 0), (0, 8 - SSD_CONV), (0, 0)))

    w_o = w_out.astype(BF16)
    w_r = _pad_lanes(jnp.concatenate(
        [router_group_w, router_expert_w.transpose(0, 2, 1, 3).reshape(depth, D, N_EXPERTS)], -1))
    b_r = _pad_lanes(jnp.concatenate(
        [router_group_b, router_expert_b.reshape(depth, N_EXPERTS)], -1).reshape(depth, 1, -1))
    e_g = expert_w_gate.astype(BF16)
    e_u = expert_w_up.astype(BF16)
    e_d = expert_w_down.astype(BF16)
    row = lambda a, l: a[l].reshape(1, -1)

    x2 = x.reshape(T, D)
    for l in range(depth):
        p_mla, p_ml, p_ssd = _inproj(x2, w_a[l], w_b[l], w_c[l])
        q, k, v = _mla_prep(p_mla, cos_t, sin_t, row(mla_q_norm, l), row(mla_kv_norm, l), wq[l], wk[l], wv[l])
        y_a = _attention(q.reshape(B, S, -1), k.reshape(B, S, -1), v.reshape(B, S, -1))
        y_b = _mlstm(p_ml.reshape(B, S, -1), gate_bias[l], row(mlstm_norm, l))
        y_c = _ssd(p_ssd.reshape(B, S, -1), conv_w[l], row(ssd_conv_b, l), dt_bias[l], a_log[l],
                   d_skip[l], row(ssd_norm, l))
        x1, cmb = _outproj(alpha, x2, y_a.reshape(T, -1), y_b.reshape(T, -1), y_c.reshape(T, -1),
                           w_o[l], row(ln1_g, l), row(ln1_b, l), w_r[l], b_r[l])
        x2 = _moe(alpha, x1, cmb, e_g[l], e_u[l], e_d[l], row(ln2_g, l), row(ln2_b, l))
    return x2.reshape(B, S, D)
```

```python
import functools

import jax
import jax.numpy as jnp
from jax import lax
from jax.experimental import pallas as pl
from jax.experimental.pallas import tpu as pltpu

F32 = jnp.float32
BF16 = jnp.bfloat16
HIGHEST = lax.Precision.HIGHEST

LANES = 128
VMEM_LIMIT = 56 * 1024 * 1024

MLA_HEADS = 8
MLA_NOPE = 64
MLA_ROPE = 32
MLA_V = 64
MLA_Q_RANK = 256
MLA_KV_RANK = 128
ROPE_BASE = 10000.0
MLSTM_HEADS = 4
MLSTM_DH = 64
SSD_HEADS = 4
SSD_HEADDIM = 64
SSD_GROUPS = 2
SSD_STATE = 128
SSD_CONV = 5
N_GROUPS = 4
EXPERTS_PER_GROUP = 4
N_EXPERTS = 16
M_INIT = -1e30

MLA_OUT = MLA_HEADS * MLA_V
MLSTM_W = MLSTM_HEADS * MLSTM_DH
SSD_INNER = SSD_HEADS * SSD_HEADDIM
SSD_CONV_DIM = SSD_INNER + 2 * SSD_GROUPS * SSD_STATE
CHUNK = 128

P_MLA_W = 512
P_ML_W = 4 * MLSTM_W + LANES
P_SSD_W = SSD_INNER + SSD_CONV_DIM + LANES


def _cparams(*sem):
    return pltpu.CompilerParams(dimension_semantics=sem, vmem_limit_bytes=VMEM_LIMIT)


def _dot(a, b):
    return jnp.dot(a, b, preferred_element_type=F32)


def _dot_nt(a, b):
    return lax.dot_general(a, b, (((1,), (1,)), ((), ())), preferred_element_type=F32)


def _dot_tn(a, b):
    return lax.dot_general(a, b, (((0,), (0,)), ((), ())), preferred_element_type=F32)


def _dot_f32(a, b):
    return jnp.dot(a, b, preferred_element_type=F32, precision=HIGHEST)


def _split3(x):
    hi = x.astype(BF16)
    r = x - hi.astype(F32)
    mid = r.astype(BF16)
    lo = (r - mid.astype(F32)).astype(BF16)
    return hi, mid, lo


def _cumsum_rows(tri_b16, x):
    hi, mid, lo = _split3(x)
    c = _dot(tri_b16, jnp.concatenate([hi, mid, lo], axis=1))
    return c[:, 2 * LANES:3 * LANES] + c[:, LANES:2 * LANES] + c[:, 0:LANES]


def _softplus(x):
    return jnp.maximum(x, 0.0) + jnp.log1p(jnp.exp(-jnp.abs(x)))


def _sigmoid(x):
    return 1.0 / (1.0 + jnp.exp(-x))


def _silu(x):
    return x * _sigmoid(x)


def _inproj_kernel(x_ref, wa_ref, wq_ref, wg_ref, wc_ref, pa_ref, pq_ref, pg_ref, pc_ref):
    x = x_ref[...].astype(BF16)
    pa_ref[...] = _dot(x, wa_ref[...])
    pq_ref[...] = _dot(x, wq_ref[...]).astype(pq_ref.dtype)
    pg_ref[...] = _dot(x, wg_ref[...])
    pc_ref[...] = _dot(x, wc_ref[...])


def _inproj(x2, wa, wq, wg, wc, tm=512):
    T, D = x2.shape
    row = lambda w: pl.BlockSpec((tm, w), lambda i: (i, 0))
    full = lambda a: pl.BlockSpec(a.shape, lambda i: (0, 0))
    return pl.pallas_call(
        _inproj_kernel,
        grid=(T // tm,),
        in_specs=[row(D), full(wa), full(wq), full(wg), full(wc)],
        out_specs=[row(P_MLA_W), row(3 * MLSTM_W), row(MLSTM_W + LANES), row(P_SSD_W)],
        out_shape=[jax.ShapeDtypeStruct((T, P_MLA_W), F32),
                   jax.ShapeDtypeStruct((T, 3 * MLSTM_W), BF16),
                   jax.ShapeDtypeStruct((T, MLSTM_W + LANES), F32),
                   jax.ShapeDtypeStruct((T, P_SSD_W), F32)],
        compiler_params=_cparams("parallel"),
        name="inproj",
    )(x2, wa, wq, wg, wc)


def _mla_prep_kernel(p_ref, cos_ref, sin_ref, qn_ref, kvn_ref, wq_ref, wk_ref, wv_ref,
                     q_ref, k_ref, v_ref):
    p = p_ref[...]
    cq = p[:, 0:MLA_Q_RANK]
    ckv = p[:, MLA_Q_RANK:MLA_Q_RANK + MLA_KV_RANK]
    krb = p[:, MLA_Q_RANK + MLA_KV_RANK:P_MLA_W]
    cqn = (cq * lax.rsqrt(jnp.mean(cq * cq, -1, keepdims=True) + 1e-6) * qn_ref[...]).astype(BF16)
    ckn = (ckv * lax.rsqrt(jnp.mean(ckv * ckv, -1, keepdims=True) + 1e-6) * kvn_ref[...]).astype(BF16)
    q = _dot(cqn, wq_ref[...])
    k = _dot(ckn, wk_ref[...])
    v_ref[...] = _dot(ckn, wv_ref[...]).astype(v_ref.dtype)
    cos = cos_ref[...]
    sin = sin_ref[...]
    lane = lax.broadcasted_iota(jnp.int32, cos.shape, 1)
    first_half = lane < MLA_NOPE + MLA_ROPE // 2

    def rope(t):
        partner = jnp.where(first_half, pltpu.roll(t, LANES - MLA_ROPE // 2, 1),
                            pltpu.roll(t, MLA_ROPE // 2, 1))
        return t * cos + partner * sin

    scale = (MLA_NOPE + MLA_ROPE) ** -0.5
    kr = rope(krb)
    for h in range(MLA_HEADS):
        sl = slice(h * LANES, (h + 1) * LANES)
        q_ref[:, sl] = (rope(q[:, sl]) * scale).astype(q_ref.dtype)
        k_ref[:, sl] = (k[:, sl] + kr).astype(k_ref.dtype)


def _mla_prep(p_mla, cos, sin, qn, kvn, wq, wk, wv, tm=512):
    T = p_mla.shape[0]
    row = lambda w: pl.BlockSpec((tm, w), lambda i: (i, 0))
    full = lambda a: pl.BlockSpec(a.shape, lambda i: (0, 0))
    HW = MLA_HEADS * LANES
    return pl.pallas_call(
        _mla_prep_kernel,
        grid=(T // tm,),
        in_specs=[row(P_MLA_W), row(LANES), row(LANES), full(qn), full(kvn), full(wq), full(wk), full(wv)],
        out_specs=[row(HW), row(HW), row(MLA_OUT)],
        out_shape=[jax.ShapeDtypeStruct((T, HW), BF16), jax.ShapeDtypeStruct((T, HW), BF16),
                   jax.ShapeDtypeStruct((T, MLA_OUT), BF16)],
        compiler_params=_cparams("parallel"),
        name="mla_prep",
    )(p_mla, cos, sin, qn, kvn, wq, wk, wv)


def _attn_kernel(q_ref, k_ref, v_ref, o_ref):
    v = v_ref[0]
    outs = []
    for hh in range(2):
        sl = slice(hh * LANES, (hh + 1) * LANES)
        s = _dot_nt(q_ref[0, :, sl], k_ref[0, :, sl])
        m = jnp.max(s, -1, keepdims=True)
        p = jnp.exp(s - m)
        l = jnp.sum(p, -1, keepdims=True)
        outs.append(_dot(p.astype(BF16), v) / l)
    lane = lax.broadcasted_iota(jnp.int32, outs[0].shape, 1)
    o_ref[0] = jnp.where(lane < MLA_V, outs[0], outs[1]).astype(o_ref.dtype)


def _attention(q, k, v, tq=512):
    B, S, _ = q.shape
    tq = min(tq, S)
    return pl.pallas_call(
        _attn_kernel,
        grid=(B, MLA_HEADS // 2, S // tq),
        in_specs=[pl.BlockSpec((1, tq, 2 * LANES), lambda b, j, i: (b, i, j)),
                  pl.BlockSpec((1, S, 2 * LANES), lambda b, j, i: (b, 0, j)),
                  pl.BlockSpec((1, S, LANES), lambda b, j, i: (b, 0, j))],
        out_specs=pl.BlockSpec((1, tq, LANES), lambda b, j, i: (b, i, j)),
        out_shape=jax.ShapeDtypeStruct((B, S, MLA_OUT), BF16),
        compiler_params=_cparams("parallel", "parallel", "arbitrary"),
        name="mla_attn",
    )(q, k, v)


_ML_FCOL = lambda d, h: 8 * d + MLSTM_HEADS + h
_ML_ICOL = lambda d, h: 8 * d + h


def _scan_max_rows(x, row, bwd_lanes):
    L = x.shape[0]
    neg = -jnp.inf
    xf, xb = x, x
    k = 1
    while k < L:
        xf = jnp.maximum(xf, jnp.where(row >= k, pltpu.roll(xf, k, 0), neg))
        xb = jnp.maximum(xb, jnp.where(row < L - k, pltpu.roll(xb, L - k, 0), neg))
        k *= 2
    return jnp.where(bwd_lanes, xb, xf)


def _mlstm_kernel(qkv_ref, og_ref, gb_ref, ng_ref, avg_ref, y_ref,
                  u_s, b_s, tot_s, mdec_s, mpf_s, mpb_s, ds_s, sp_s, st_s):
    S = qkv_ref.shape[1]
    L = CHUNK
    nc = S // L
    H = MLSTM_HEADS
    SW = 2 * LANES
    ti = lax.broadcasted_iota(jnp.int32, (L, L), 0)
    si = lax.broadcasted_iota(jnp.int32, (L, L), 1)
    tri = (si <= ti, si >= ti)
    tri_b16 = tri[0].astype(BF16)
    lane = lax.broadcasted_iota(jnp.int32, (L, LANES), 1)
    row = lax.broadcasted_iota(jnp.int32, (L, LANES), 0)
    lane1 = lax.broadcasted_iota(jnp.int32, (1, LANES), 1)
    bwd_lanes = lane >= 2 * H
    bwd_lanes1 = lane1 >= 2 * H
    lo = lane < MLSTM_DH
    gbias = gb_ref[...]
    ones_b16 = jnp.ones((L, LANES), BF16)
    zero_b16 = jnp.zeros((L, LANES), BF16)
    QO, KO, VO = 0, MLSTM_W, 2 * MLSTM_W
    OO, GO = 0, MLSTM_W

    def head_slices(rows, j, e, off):
        sel = lo if e == 0 else ~lo
        return jnp.where(sel, qkv_ref[0, rows, off + j * LANES:off + (j + 1) * LANES], zero_b16)

    def phase_a(c, carry):
        rows = pl.ds(pl.multiple_of(c * L, L), L)
        G = og_ref[0, rows, GO:GO + LANES] + gbias
        LF = -_softplus(-G)
        pre = _cumsum_rows(tri_b16, LF)
        tot = pre[L - 1:L, :]
        Bm = jnp.where(bwd_lanes, tot - pre + LF, pre)
        U = pltpu.roll(G, H, 1) - Bm
        u_s[rows, :] = U
        b_s[rows, :] = Bm
        dec = tot + U
        mdec = jnp.max(dec, 0, keepdims=True)
        W = jnp.exp(dec - mdec)
        tot_s[c] = tot
        mdec_s[c] = mdec
        for j in range(H // 2):
            for e in range(2):
                h = 2 * j + e
                km = head_slices(rows, j, e, KO)
                vm = head_slices(rows, j, e, VO).astype(F32)
                parts = []
                for d in range(2):
                    wb = jnp.broadcast_to(W[:, _ML_FCOL(d, h):_ML_FCOL(d, h) + 1], (L, LANES))
                    parts += [wb * vm, wb]
                wv = jnp.concatenate(parts, axis=1).astype(BF16)
                ds_s[c * H + h] = _dot_tn(km, wv)
        return carry

    lax.fori_loop(0, nc, phase_a, 0, unroll=2)

    st_s[...] = jnp.zeros(st_s.shape, F32)

    def phase_b(i, m):
        cf, cb = i, nc - 1 - i
        totm = jnp.where(bwd_lanes1, tot_s[cb], tot_s[cf])
        mdecm = jnp.where(bwd_lanes1, mdec_s[cb], mdec_s[cf])
        mpf_s[cf] = m
        mpb_s[cb] = m
        m_new = jnp.maximum(totm + m, mdecm)
        w_c = jnp.exp(totm + m - m_new)
        w_d = jnp.exp(mdecm - m_new)
        for h in range(H):
            for d, c in ((0, cf), (1, cb)):
                part = slice(d * SW, (d + 1) * SW)
                col = slice(_ML_FCOL(d, h), _ML_FCOL(d, h) + 1)
                st = st_s[h, :, part]
                sp_s[c * H + h, :, part] = st.astype(BF16)
                st_s[h, :, part] = w_c[:, col] * st + w_d[:, col] * ds_s[c * H + h, :, part]
        return m_new

    lax.fori_loop(0, nc, phase_b, jnp.full((1, LANES), M_INIT, F32))

    ng = ng_ref[...]
    avg = avg_ref[...]

    def head_mean(t):
        hi = t.astype(BF16)
        lo_part = (t - hi.astype(F32)).astype(BF16)
        return _dot(jnp.concatenate([hi, lo_part], axis=1), avg)

    def phase_c(c, carry):
        rows = pl.ds(pl.multiple_of(c * L, L), L)
        U = u_s[rows, :]
        Bm = b_s[rows, :]
        UT = U.T
        m_prev = jnp.where(bwd_lanes1, mpb_s[c], mpf_s[c])
        Gm = Bm + m_prev
        Mt = jnp.maximum(Gm, Bm + _scan_max_rows(U, row, bwd_lanes))
        Z = Bm - Mt
        WI = jnp.exp(Gm - Mt)
        FL = jnp.exp(-Mt)
        for j in range(H // 2):
            pair = slice(j * LANES, (j + 1) * LANES)
            kb = qkv_ref[0, rows, KO + j * LANES:KO + (j + 1) * LANES]
            hsum = []
            for e in range(2):
                h = 2 * j + e
                qm = head_slices(rows, j, e, QO)
                vaug = jnp.concatenate([head_slices(rows, j, e, VO), ones_b16], axis=1)
                qk = _dot_nt(qm, kb)
                inter = _dot(qm, sp_s[c * H + h])
                ps = []
                for d in range(2):
                    fc = _ML_FCOL(d, h)
                    e_ts = jnp.where(tri[d], UT[fc:fc + 1, :], -jnp.inf) + Z[:, fc:fc + 1]
                    ps.append((qk * jnp.exp(e_ts)).astype(BF16))
                pv = _dot(jnp.concatenate(ps, axis=0), vaug)
                hs = None
                for d in range(2):
                    fc = _ML_FCOL(d, h)
                    nd = pv[d * L:(d + 1) * L] + WI[:, fc:fc + 1] * inter[:, d * SW:(d + 1) * SW]
                    hd = nd[:, 0:LANES] / jnp.maximum(jnp.abs(nd[:, LANES:SW]), FL[:, fc:fc + 1])
                    hs = hd if hs is None else hs + hd
                hsum.append(hs)
            hp = jnp.where(lo, hsum[0], hsum[1])
            dlt = hp - head_mean(hp)
            var = head_mean(dlt * dlt)
            hn = dlt * lax.rsqrt(var + 1e-5) * ng[:, pair]
            o = og_ref[0, rows, OO + j * LANES:OO + (j + 1) * LANES]
            y_ref[0, rows, pair] = (_sigmoid(o) * hn).astype(y_ref.dtype)
        return carry

    lax.fori_loop(0, nc, phase_c, 0, unroll=2)


def _mlstm(p_qkv, p_og, gate_bias, norm_g, avg):
    B, S, _ = p_qkv.shape
    nc = S // CHUNK
    H = MLSTM_HEADS
    vec = lambda: pltpu.VMEM((nc, 1, LANES), F32)
    return pl.pallas_call(
        _mlstm_kernel,
        grid=(B,),
        in_specs=[pl.BlockSpec((1, S, 3 * MLSTM_W), lambda b: (b, 0, 0)),
                  pl.BlockSpec((1, S, MLSTM_W + LANES), lambda b: (b, 0, 0)),
                  pl.BlockSpec((1, LANES), lambda b: (0, 0)),
                  pl.BlockSpec((1, MLSTM_W), lambda b: (0, 0)),
                  pl.BlockSpec(avg.shape, lambda b: (0, 0))],
        out_specs=pl.BlockSpec((1, S, MLSTM_W), lambda b: (b, 0, 0)),
        out_shape=jax.ShapeDtypeStruct((B, S, MLSTM_W), BF16),
        scratch_shapes=[pltpu.VMEM((S, LANES), F32), pltpu.VMEM((S, LANES), F32),
                        vec(), vec(), vec(), vec(),
                        pltpu.VMEM((nc * H, LANES, 4 * LANES), F32),
                        pltpu.VMEM((nc * H, LANES, 4 * LANES), BF16),
                        pltpu.VMEM((H, LANES, 4 * LANES), F32)],
        compiler_params=_cparams("parallel"),
        name="mlstm",
    )(p_qkv, p_og, gate_bias, norm_g, avg)


def _ssd_kernel(p_ref, cw_ref, cb_ref, dtb_ref, alog_ref, dsk_ref, ng_ref, y_ref,
                xpad_ref, xc_ref, y1_ref, y2_ref, st_ref):
    S = p_ref.shape[1]
    L = CHUNK
    nc = S // L
    PAD = 8
    ZO, XO, DO = 0, SSD_INNER, SSD_INNER + SSD_CONV_DIM

    zpad = jnp.zeros((PAD, SSD_CONV_DIM), F32)
    xpad_ref[0:PAD, :] = zpad
    xpad_ref[PAD + S:PAD + S + PAD, :] = zpad
    R = 256
    for r in range(S // R):
        xpad_ref[PAD + r * R:PAD + (r + 1) * R, :] = p_ref[0, r * R:(r + 1) * R, XO:XO + SSD_CONV_DIM]
    half = SSD_CONV // 2
    for r in range(S // R):
        acc = jnp.zeros((R, SSD_CONV_DIM), F32) + cb_ref[...]
        for kk in range(SSD_CONV):
            off = PAD + r * R + kk - half
            acc = acc + xpad_ref[off:off + R, :] * cw_ref[kk:kk + 1, :]
        xc_ref[r * R:(r + 1) * R, :] = _silu(acc)

    st_ref[...] = jnp.zeros(st_ref.shape, F32)
    ti = lax.broadcasted_iota(jnp.int32, (L, L), 0)
    si = lax.broadcasted_iota(jnp.int32, (L, L), 1)
    tri = (si <= ti, si >= ti)
    tri_f = tuple(t.astype(F32) for t in tri)
    lane = lax.broadcasted_iota(jnp.int32, (L, LANES), 1)
    lo = lane < SSD_HEADDIM
    a_neg = -jnp.exp(alog_ref[...])
    dtb = dtb_ref[...]
    BO, CO = SSD_INNER, SSD_INNER + SSD_GROUPS * SSD_STATE

    def dt_cols(rows):
        dt = _softplus(p_ref[0, rows, DO:DO + LANES] + dtb)
        return dt, dt * a_neg

    def chunk_step(c, carry):
        rf = pl.ds(pl.multiple_of(c * L, L), L)
        rb = pl.ds(pl.multiple_of((nc - 1 - c) * L, L), L)
        dt, dta = dt_cols(rf)
        cs = (_dot_f32(tri_f[0], dta), _dot_f32(tri_f[1], dta))
        csr = (cs[0].T, cs[1].T)
        dtr = dt.T
        for g in range(SSD_GROUPS):
            pair = slice(g * LANES, (g + 1) * LANES)
            xp = xc_ref[rf, pair]
            Bg = xc_ref[rf, BO + g * LANES:BO + (g + 1) * LANES].astype(BF16)
            Cg = xc_ref[rf, CO + g * LANES:CO + (g + 1) * LANES].astype(BF16)
            CB = _dot_nt(Cg, Bg)
            xb = xp.astype(BF16)
            ys = []
            for e in range(2):
                h = SSD_GROUPS * g + e
                W = jnp.zeros((L, L), F32)
                for d in range(2):
                    col = d * SSD_HEADS + h
                    arg = jnp.where(tri[d], cs[d][:, col:col + 1] - csr[d][col:col + 1, :], -jnp.inf)
                    W = W + jnp.exp(arg) * dtr[col:col + 1, :]
                ys.append(_dot((CB * W).astype(BF16), xb))
            y1_ref[rf, pair] = jnp.where(lo, ys[0], ys[1])
        for d in range(2):
            rows = rf if d == 0 else rb
            if d == 0:
                dt_d, cs_d = dt, cs[0]
            else:
                dt_d, dta_d = dt_cols(rows)
                cs_d = _dot_f32(tri_f[1], dta_d)
            for g in range(SSD_GROUPS):
                pair = slice(g * LANES, (g + 1) * LANES)
                ce = d * SSD_HEADS + SSD_GROUPS * g
                xp = xc_ref[rows, pair]
                Bg = xc_ref[rows, BO + g * LANES:BO + (g + 1) * LANES].astype(BF16)
                Cg = xc_ref[rows, CO + g * LANES:CO + (g + 1) * LANES].astype(BF16)
                st = st_ref[d * SSD_GROUPS + g]
                cs2 = jnp.where(lo, cs_d[:, ce:ce + 1], cs_d[:, ce + 1:ce + 2])
                dt2 = jnp.where(lo, dt_d[:, ce:ce + 1], dt_d[:, ce + 1:ce + 2])
                yi = _dot(Cg, st.astype(BF16)) * jnp.exp(cs2)
                if d == 0:
                    y1_ref[rows, pair] += yi
                    tot = cs2[L - 1:L, :]
                else:
                    y2_ref[rows, pair] = yi
                    tot = cs2[0:1, :]
                xdt = xp * (dt2 * jnp.exp(tot - cs2))
                st_ref[d * SSD_GROUPS + g] = jnp.exp(tot) * st + _dot_tn(Bg, xdt.astype(BF16))
        return carry

    lax.fori_loop(0, nc, chunk_step, 0)

    dsk = dsk_ref[...]
    ng = ng_ref[...]

    def epilogue(r, carry):
        rows = pl.ds(pl.multiple_of(r * R, R), R)
        for g in range(SSD_GROUPS):
            pair = slice(g * LANES, (g + 1) * LANES)
            y = y1_ref[rows, pair] + y2_ref[rows, pair] + xc_ref[rows, pair] * dsk[:, pair]
            y = y * _silu(p_ref[0, rows, ZO + g * LANES:ZO + (g + 1) * LANES])
            y = y * lax.rsqrt(jnp.mean(y * y, -1, keepdims=True) + 1e-6) * ng[:, pair]
            y_ref[0, rows, pair] = y.astype(y_ref.dtype)
        return carry

    lax.fori_loop(0, S // R, epilogue, 0)


def _ssd(p_ssd, conv_w, conv_b, dt_bias, a_log, d_skip, norm_g):
    B, S, _ = p_ssd.shape
    full = lambda a: pl.BlockSpec(a.shape, lambda b: (0, 0))
    return pl.pallas_call(
        _ssd_kernel,
        grid=(B,),
        in_specs=[pl.BlockSpec((1, S, P_SSD_W), lambda b: (b, 0, 0)),
                  full(conv_w), full(conv_b), full(dt_bias), full(a_log), full(d_skip), full(norm_g)],
        out_specs=pl.BlockSpec((1, S, SSD_INNER), lambda b: (b, 0, 0)),
        out_shape=jax.ShapeDtypeStruct((B, S, SSD_INNER), BF16),
        scratch_shapes=[pltpu.VMEM((S + 16, SSD_CONV_DIM), F32), pltpu.VMEM((S, SSD_CONV_DIM), F32),
                        pltpu.VMEM((S, SSD_INNER), F32), pltpu.VMEM((S, SSD_INNER), F32),
                        pltpu.VMEM((2 * SSD_GROUPS, SSD_STATE, LANES), F32)],
        compiler_params=_cparams("parallel"),
        name="ssd",
    )(p_ssd, conv_w, conv_b, dt_bias, a_log, d_skip, norm_g)


def _layernorm(u, g, b):
    mu = jnp.mean(u, -1, keepdims=True)
    d = u - mu
    var = jnp.mean(d * d, -1, keepdims=True)
    return d * lax.rsqrt(var + 1e-5) * g + b


def _route(logits):
    lane_i = lax.broadcasted_iota(jnp.int32, logits.shape, 1)
    lane = lane_i.astype(F32)
    group_of_lane = jnp.right_shift(lane_i - N_GROUPS, 2).astype(F32)
    neg = -jnp.inf
    big = 1e6
    glm = jnp.where(lane_i < N_GROUPS, logits, neg)
    gmax = jnp.max(glm, -1, keepdims=True)
    gp = 1.0 / jnp.sum(jnp.exp(glm - gmax), -1, keepdims=True)
    gi = jnp.min(jnp.where(glm == gmax, lane, big), -1, keepdims=True)
    in_group = (lane_i >= N_GROUPS) & (lane_i < N_GROUPS + N_EXPERTS) & (group_of_lane == gi)
    elm = jnp.where(in_group, logits, neg)
    e1 = jnp.max(elm, -1, keepdims=True)
    i1 = jnp.min(jnp.where(elm == e1, lane, big), -1, keepdims=True)
    elm2 = jnp.where(lane == i1, neg, elm)
    e2 = jnp.max(elm2, -1, keepdims=True)
    i2 = jnp.min(jnp.where(elm2 == e2, lane, big), -1, keepdims=True)
    r = jnp.exp(e2 - e1)
    p1 = 1.0 / (1.0 + r)
    p2 = r / (1.0 + r)
    return jnp.where(lane == i1, gp * p1, jnp.where(lane == i2, gp * p2, 0.0))


OUT_SUB = 256


def _outproj_kernel(alpha, x_ref, ya_ref, yb_ref, yc_ref, wo_ref, g_ref, b_ref, wr_ref, br_ref,
                    x1_ref, cmb_ref):
    tm = x_ref.shape[0]
    for r in range(tm // OUT_SUB):
        rows = slice(r * OUT_SUB, (r + 1) * OUT_SUB)
        mix = (_dot(ya_ref[rows, :], wo_ref[0:MLA_OUT, :])
               + _dot(yb_ref[rows, :], wo_ref[MLA_OUT:MLA_OUT + MLSTM_W, :])
               + _dot(yc_ref[rows, :], wo_ref[MLA_OUT + MLSTM_W:, :]))
        x1 = _layernorm(alpha * x_ref[rows, :] + mix, g_ref[...], b_ref[...])
        x1_ref[rows, :] = x1
        x_hi = x1.astype(BF16)
        x_lo = (x1 - x_hi.astype(F32)).astype(BF16)
        a = _dot(x_hi, wr_ref[...])
        c = _dot(x_lo, wr_ref[...])
        logits = (a[:, 0:LANES] + (a[:, LANES:] + c[:, 0:LANES]) + c[:, LANES:]) + br_ref[...]
        cmb_ref[rows, :] = _route(logits)


def _outproj(alpha, x2, ya, yb, yc, wo, g, b, wr, br, tm=1024):
    T, D = x2.shape
    row = lambda w: pl.BlockSpec((tm, w), lambda i: (i, 0))
    full = lambda a: pl.BlockSpec(a.shape, lambda i: (0, 0))
    return pl.pallas_call(
        functools.partial(_outproj_kernel, alpha),
        grid=(T // tm,),
        in_specs=[row(D), row(MLA_OUT), row(MLSTM_W), row(SSD_INNER), full(wo), full(g), full(b),
                  full(wr), full(br)],
        out_specs=[row(D), row(LANES)],
        out_shape=[jax.ShapeDtypeStruct((T, D), F32), jax.ShapeDtypeStruct((T, LANES), F32)],
        compiler_params=_cparams("parallel"),
        name="outproj_ln_router",
    )(x2, ya, yb, yc, wo, g, b, wr, br)


def _moe_kernel(alpha, x_ref, cmb_ref, wg_ref, wu_ref, wd_ref, g_ref, b_ref, o_ref, xb_ref, acc_ref):
    grp = pl.program_id(1)

    @pl.when(grp == 0)
    def _():
        xb_ref[...] = x_ref[...].astype(BF16)
        acc_ref[...] = jnp.zeros(acc_ref.shape, F32)

    xb = xb_ref[...]
    cmb = cmb_ref[...]
    lane = lax.broadcasted_iota(jnp.int32, cmb.shape, 1)
    for e in range(EXPERTS_PER_GROUP):
        col = N_GROUPS + grp * EXPERTS_PER_GROUP + e
        c = jnp.sum(jnp.where(lane == col, cmb, 0.0), -1, keepdims=True)
        h = _silu(_dot(xb, wg_ref[e])) * _dot(xb, wu_ref[e]) * c
        acc_ref[...] += _dot(h.astype(BF16), wd_ref[e])

    @pl.when(grp == N_GROUPS - 1)
    def _():
        o_ref[...] = _layernorm(alpha * x_ref[...] + acc_ref[...], g_ref[...], b_ref[...])


def _moe(alpha, x1, cmb, wg, wu, wd, g, b, tm=1024):
    T, D = x1.shape
    F = wg.shape[-1]
    E = EXPERTS_PER_GROUP
    return pl.pallas_call(
        functools.partial(_moe_kernel, alpha),
        grid=(T // tm, N_GROUPS),
        in_specs=[pl.BlockSpec((tm, D), lambda i, e: (i, 0)),
                  pl.BlockSpec((tm, LANES), lambda i, e: (i, 0)),
                  pl.BlockSpec((E, D, F), lambda i, e: (e, 0, 0)),
                  pl.BlockSpec((E, D, F), lambda i, e: (e, 0, 0)),
                  pl.BlockSpec((E, F, D), lambda i, e: (e, 0, 0)),
                  pl.BlockSpec((1, D), lambda i, e: (0, 0)),
                  pl.BlockSpec((1, D), lambda i, e: (0, 0))],
        out_specs=pl.BlockSpec((tm, D), lambda i, e: (i, 0)),
        out_shape=jax.ShapeDtypeStruct((T, D), F32),
        scratch_shapes=[pltpu.VMEM((tm, D), BF16), pltpu.VMEM((tm, D), F32)],
        compiler_params=_cparams("parallel", "arbitrary"),
        name="moe_ln",
    )(x1, cmb, wg, wu, wd, g, b)


def _pad_lanes(a, width=LANES):
    return jnp.pad(a, [(0, 0)] * (a.ndim - 1) + [(0, width - a.shape[-1])])


def kernel(x, positions, w_in, mla_q_norm, mla_kv_norm, mla_w_uq, mla_w_ukv, mlstm_gate_bias, mlstm_norm,
           ssd_conv_w, ssd_conv_b, ssd_dt_bias, ssd_a_log, ssd_d, ssd_norm, w_out, ln1_g, ln1_b,
           router_group_w, router_group_b, router_expert_w, router_expert_b,
           expert_w_gate, expert_w_up, expert_w_down, ln2_g, ln2_b):
    B, S, D = x.shape
    depth = w_in.shape[0]
    T = B * S
    alpha = (2 * depth) ** 0.25

    inv = ROPE_BASE ** (-jnp.arange(0, MLA_ROPE, 2, dtype=F32) / MLA_ROPE)
    ang = positions.astype(F32).reshape(T, 1) * inv
    cos, sin = jnp.cos(ang), jnp.sin(ang)
    ones = jnp.ones((T, MLA_NOPE), F32)
    zeros = jnp.zeros((T, MLA_NOPE), F32)
    pad = jnp.zeros((T, LANES - MLA_NOPE - MLA_ROPE), F32)
    cos_t = jnp.concatenate([ones, cos, cos, pad], -1)
    sin_t = jnp.concatenate([zeros, -sin, sin, pad], -1)

    o = [0]
    for s in (MLA_Q_RANK, MLA_KV_RANK, MLA_ROPE, MLSTM_W, MLSTM_W, MLSTM_W, MLSTM_W, 4 * MLSTM_HEADS,
              SSD_INNER, SSD_CONV_DIM, 2 * SSD_HEADS):
        o.append(o[-1] + s)
    zc = lambda n: jnp.zeros((depth, D, n), w_in.dtype)
    w_a = jnp.concatenate([w_in[..., o[0]:o[2]], zc(MLA_NOPE), w_in[..., o[2]:o[3]],
                           zc(LANES - MLA_NOPE - MLA_ROPE)], -1).astype(BF16)
    w_qkv = jnp.concatenate([w_in[..., o[3]:o[4]], w_in[..., o[4]:o[5]] * (MLSTM_DH ** -0.5),
                             w_in[..., o[5]:o[6]]], -1).astype(BF16)
    w_og = jnp.concatenate([w_in[..., o[6]:o[8]], zc(LANES - 4 * MLSTM_HEADS)], -1).astype(BF16)
    head_of_lane = jnp.arange(LANES) // MLSTM_DH
    avg_blk = (head_of_lane[:, None] == head_of_lane[None, :]).astype(BF16) * (1.0 / MLSTM_DH)
    ml_avg = jnp.concatenate([avg_blk, avg_blk], 0)
    w_c = jnp.concatenate([w_in[..., o[8]:o[11]], zc(LANES - 2 * SSD_HEADS)], -1).astype(BF16)

    wq = mla_w_uq.reshape(depth, MLA_Q_RANK, MLA_HEADS, MLA_NOPE + MLA_ROPE)
    wq = _pad_lanes(wq).reshape(depth, MLA_Q_RANK, MLA_HEADS * LANES).astype(BF16)
    wkv = mla_w_ukv.reshape(depth, MLA_KV_RANK, MLA_HEADS, MLA_NOPE + MLA_V)
    wk = _pad_lanes(wkv[..., :MLA_NOPE]).reshape(depth, MLA_KV_RANK, MLA_HEADS * LANES).astype(BF16)
    wv = wkv[..., MLA_NOPE:].reshape(depth, MLA_KV_RANK, MLA_OUT).astype(BF16)

    gate_bias = _pad_lanes(mlstm_gate_bias.reshape(depth, 1, 4 * MLSTM_HEADS))
    dt_bias = _pad_lanes(ssd_dt_bias.reshape(depth, 1, 2 * SSD_HEADS))
    a_log = _pad_lanes(ssd_a_log.reshape(depth, 1, 2 * SSD_HEADS))
    d_skip = jnp.repeat(ssd_d, SSD_HEADDIM, axis=-1).reshape(depth, 1, SSD_INNER)
    conv_w = jnp.pad(ssd_conv_w, ((0, 0), (0, 8 - SSD_CONV), (0, 0)))

    w_o = w_out.astype(BF16)
    w_r = _pad_lanes(jnp.concatenate(
        [router_group_w, router_expert_w.transpose(0, 2, 1, 3).reshape(depth, D, N_EXPERTS)], -1))
    w_r_hi = w_r.astype(BF16)
    w_r2 = jnp.concatenate([w_r_hi, (w_r - w_r_hi.astype(F32)).astype(BF16)], -1)
    b_r = _pad_lanes(jnp.concatenate(
        [router_group_b, router_expert_b.reshape(depth, N_EXPERTS)], -1).reshape(depth, 1, -1))
    e_g = expert_w_gate.astype(BF16)
    e_u = expert_w_up.astype(BF16)
    e_d = expert_w_down.astype(BF16)
    row = lambda a, l: a[l].reshape(1, -1)

    x2 = x.reshape(T, D)
    for l in range(depth):
        p_mla, p_qkv, p_og, p_ssd = _inproj(x2, w_a[l], w_qkv[l], w_og[l], w_c[l])
        q, k, v = _mla_prep(p_mla, cos_t, sin_t, row(mla_q_norm, l), row(mla_kv_norm, l), wq[l], wk[l], wv[l])
        y_a = _attention(q.reshape(B, S, -1), k.reshape(B, S, -1), v.reshape(B, S, -1))
        y_b = _mlstm(p_qkv.reshape(B, S, -1), p_og.reshape(B, S, -1), gate_bias[l], row(mlstm_norm, l), ml_avg)
        y_c = _ssd(p_ssd.reshape(B, S, -1), conv_w[l], row(ssd_conv_b, l), dt_bias[l], a_log[l],
                   d_skip[l], row(ssd_norm, l))
        x1, cmb = _outproj(alpha, x2, y_a.reshape(T, -1), y_b.reshape(T, -1), y_c.reshape(T, -1),
                           w_o[l], row(ln1_g, l), row(ln1_b, l), w_r2[l], b_r[l])
        x2 = _moe(alpha, x1, cmb, e_g[l], e_u[l], e_d[l], row(ln2_g, l), row(ln2_b, l))
    return x2.reshape(B, S, D)
```

```python
import functools

import jax
import jax.numpy as jnp
from jax import lax
from jax.experimental import pallas as pl
from jax.experimental.pallas import tpu as pltpu

F32 = jnp.float32
BF16 = jnp.bfloat16
HIGHEST = lax.Precision.HIGHEST

LANES = 128
VMEM_LIMIT = 56 * 1024 * 1024

MLA_HEADS = 8
MLA_NOPE = 64
MLA_ROPE = 32
MLA_V = 64
MLA_Q_RANK = 256
MLA_KV_RANK = 128
ROPE_BASE = 10000.0
MLSTM_HEADS = 4
MLSTM_DH = 64
SSD_HEADS = 4
SSD_HEADDIM = 64
SSD_GROUPS = 2
SSD_STATE = 128
SSD_CONV = 5
N_GROUPS = 4
EXPERTS_PER_GROUP = 4
N_EXPERTS = 16
M_INIT = -1e30
LOG2E = 1.4426950408889634

MLA_OUT = MLA_HEADS * MLA_V
MLSTM_W = MLSTM_HEADS * MLSTM_DH
SSD_INNER = SSD_HEADS * SSD_HEADDIM
SSD_CONV_DIM = SSD_INNER + 2 * SSD_GROUPS * SSD_STATE
CHUNK = 128

P_MLA_W = 512
P_ML_W = 4 * MLSTM_W + LANES
P_SSD_W = SSD_INNER + SSD_CONV_DIM + LANES


def _cparams(*sem):
    return pltpu.CompilerParams(dimension_semantics=sem, vmem_limit_bytes=VMEM_LIMIT)


def _dot(a, b):
    return jnp.dot(a, b, preferred_element_type=F32)


def _dot_nt(a, b):
    return lax.dot_general(a, b, (((1,), (1,)), ((), ())), preferred_element_type=F32)


def _dot_tn(a, b):
    return lax.dot_general(a, b, (((0,), (0,)), ((), ())), preferred_element_type=F32)


def _dot_f32(a, b):
    return jnp.dot(a, b, preferred_element_type=F32, precision=HIGHEST)


def _split3(x):
    hi = x.astype(BF16)
    r = x - hi.astype(F32)
    mid = r.astype(BF16)
    lo = (r - mid.astype(F32)).astype(BF16)
    return hi, mid, lo


def _cumsum_rows(tri_b16, x):
    hi, mid, lo = _split3(x)
    c = _dot(tri_b16, jnp.concatenate([hi, mid, lo], axis=1))
    return c[:, 2 * LANES:3 * LANES] + c[:, LANES:2 * LANES] + c[:, 0:LANES]


def _softplus(x):
    return jnp.maximum(x, 0.0) + jnp.log1p(jnp.exp(-jnp.abs(x)))


def _sigmoid(x):
    return 1.0 / (1.0 + jnp.exp(-x))


def _silu(x):
    return x * _sigmoid(x)


def _inproj_kernel(x_ref, wa_ref, wq_ref, wg_ref, wc_ref, pa_ref, pq_ref, pg_ref, pc_ref):
    x = x_ref[...].astype(BF16)
    pa_ref[...] = _dot(x, wa_ref[...])
    pq_ref[...] = _dot(x, wq_ref[...]).astype(pq_ref.dtype)
    pg_ref[...] = _dot(x, wg_ref[...])
    pc_ref[...] = _dot(x, wc_ref[...])


def _inproj(x2, wa, wq, wg, wc, tm=512):
    T, D = x2.shape
    row = lambda w: pl.BlockSpec((tm, w), lambda i: (i, 0))
    full = lambda a: pl.BlockSpec(a.shape, lambda i: (0, 0))
    return pl.pallas_call(
        _inproj_kernel,
        grid=(T // tm,),
        in_specs=[row(D), full(wa), full(wq), full(wg), full(wc)],
        out_specs=[row(P_MLA_W), row(3 * MLSTM_W), row(MLSTM_W + LANES), row(P_SSD_W)],
        out_shape=[jax.ShapeDtypeStruct((T, P_MLA_W), F32),
                   jax.ShapeDtypeStruct((T, 3 * MLSTM_W), BF16),
                   jax.ShapeDtypeStruct((T, MLSTM_W + LANES), F32),
                   jax.ShapeDtypeStruct((T, P_SSD_W), F32)],
        compiler_params=_cparams("parallel"),
        name="inproj",
    )(x2, wa, wq, wg, wc)


def _mla_prep_kernel(p_ref, cos_ref, sin_ref, qn_ref, kvn_ref, wq_ref, wk_ref, wv_ref,
                     q_ref, k_ref, v_ref):
    p = p_ref[...]
    cq = p[:, 0:MLA_Q_RANK]
    ckv = p[:, MLA_Q_RANK:MLA_Q_RANK + MLA_KV_RANK]
    krb = p[:, MLA_Q_RANK + MLA_KV_RANK:P_MLA_W]
    cqn = (cq * lax.rsqrt(jnp.mean(cq * cq, -1, keepdims=True) + 1e-6) * qn_ref[...]).astype(BF16)
    ckn = (ckv * lax.rsqrt(jnp.mean(ckv * ckv, -1, keepdims=True) + 1e-6) * kvn_ref[...]).astype(BF16)
    q = _dot(cqn, wq_ref[...])
    k = _dot(ckn, wk_ref[...])
    v = _dot(ckn, wv_ref[...])
    vlane = lax.broadcasted_iota(jnp.int32, v.shape, 1)
    v_ref[...] = jnp.where(vlane % (2 * LANES) >= LANES, 1.0, v).astype(v_ref.dtype)
    cos = cos_ref[...]
    sin = sin_ref[...]
    lane = lax.broadcasted_iota(jnp.int32, cos.shape, 1)
    first_half = lane < MLA_NOPE + MLA_ROPE // 2

    def rope(t):
        partner = jnp.where(first_half, pltpu.roll(t, LANES - MLA_ROPE // 2, 1),
                            pltpu.roll(t, MLA_ROPE // 2, 1))
        return t * cos + partner * sin

    scale = (MLA_NOPE + MLA_ROPE) ** -0.5 * LOG2E
    kr = rope(krb)
    for h in range(MLA_HEADS):
        sl = slice(h * LANES, (h + 1) * LANES)
        q_ref[:, sl] = (rope(q[:, sl]) * scale).astype(q_ref.dtype)
        k_ref[:, sl] = (k[:, sl] + kr).astype(k_ref.dtype)


def _mla_prep(p_mla, cos, sin, qn, kvn, wq, wk, wv, tm=512):
    T = p_mla.shape[0]
    row = lambda w: pl.BlockSpec((tm, w), lambda i: (i, 0))
    full = lambda a: pl.BlockSpec(a.shape, lambda i: (0, 0))
    HW = MLA_HEADS * LANES
    return pl.pallas_call(
        _mla_prep_kernel,
        grid=(T // tm,),
        in_specs=[row(P_MLA_W), row(LANES), row(LANES), full(qn), full(kvn), full(wq), full(wk), full(wv)],
        out_specs=[row(HW), row(HW), row(HW)],
        out_shape=[jax.ShapeDtypeStruct((T, HW), BF16)] * 3,
        compiler_params=_cparams("parallel"),
        name="mla_prep",
    )(p_mla, cos, sin, qn, kvn, wq, wk, wv)


ATTN_SUB = 128


def _attn_kernel(q_ref, k_ref, v_ref, o_ref):
    tq = q_ref.shape[1]
    lane = lax.broadcasted_iota(jnp.int32, (ATTN_SUB, LANES), 1)
    for r in range(tq // ATTN_SUB):
        rows = slice(r * ATTN_SUB, (r + 1) * ATTN_SUB)
        outs = []
        for hh in range(2):
            sl = slice(hh * LANES, (hh + 1) * LANES)
            s = _dot_nt(q_ref[0, rows, sl], k_ref[0, :, sl])
            p = jnp.exp2(s - jnp.max(s, -1, keepdims=True)).astype(BF16)
            ov = _dot(p, v_ref[0])
            outs.append(ov[:, 0:LANES] / ov[:, LANES:])
        o_ref[0, rows, :] = jnp.where(lane < MLA_V, outs[0], outs[1]).astype(o_ref.dtype)


def _attention(q, k, v, tq=512):
    B, S, _ = q.shape
    tq = min(tq, S)
    return pl.pallas_call(
        _attn_kernel,
        grid=(B, MLA_HEADS // 2, S // tq),
        in_specs=[pl.BlockSpec((1, tq, 2 * LANES), lambda b, j, i: (b, i, j)),
                  pl.BlockSpec((1, S, 2 * LANES), lambda b, j, i: (b, 0, j)),
                  pl.BlockSpec((1, S, 2 * LANES), lambda b, j, i: (b, 0, j))],
        out_specs=pl.BlockSpec((1, tq, LANES), lambda b, j, i: (b, i, j)),
        out_shape=jax.ShapeDtypeStruct((B, S, MLA_OUT), BF16),
        compiler_params=_cparams("parallel", "parallel", "arbitrary"),
        name="mla_attn",
    )(q, k, v)


_ML_FCOL = lambda d, h: 8 * d + MLSTM_HEADS + h
_ML_ICOL = lambda d, h: 8 * d + h


def _scan_max_rows(x, row, bwd_lanes):
    L = x.shape[0]
    neg = -jnp.inf
    xf, xb = x, x
    k = 1
    while k < L:
        xf = jnp.maximum(xf, jnp.where(row >= k, pltpu.roll(xf, k, 0), neg))
        xb = jnp.maximum(xb, jnp.where(row < L - k, pltpu.roll(xb, L - k, 0), neg))
        k *= 2
    return jnp.where(bwd_lanes, xb, xf)


def _mlstm_kernel(qkv_ref, og_ref, gb_ref, ng_ref, avg_ref, y_ref,
                  u_s, b_s, tot_s, mdec_s, mpf_s, mpb_s, ds_s, sp_s, st_s):
    S = qkv_ref.shape[1]
    L = CHUNK
    nc = S // L
    H = MLSTM_HEADS
    SW = 2 * LANES
    ti = lax.broadcasted_iota(jnp.int32, (L, L), 0)
    si = lax.broadcasted_iota(jnp.int32, (L, L), 1)
    tri = (si <= ti, si >= ti)
    tri_b16 = tri[0].astype(BF16)
    lane = lax.broadcasted_iota(jnp.int32, (L, LANES), 1)
    row = lax.broadcasted_iota(jnp.int32, (L, LANES), 0)
    lane1 = lax.broadcasted_iota(jnp.int32, (1, LANES), 1)
    bwd_lanes = lane >= 2 * H
    bwd_lanes1 = lane1 >= 2 * H
    lo = lane < MLSTM_DH
    gbias = gb_ref[...]
    ones_b16 = jnp.ones((L, LANES), BF16)
    zero_b16 = jnp.zeros((L, LANES), BF16)
    QO, KO, VO = 0, MLSTM_W, 2 * MLSTM_W
    OO, GO = 0, MLSTM_W

    def head_slices(rows, j, e, off):
        sel = lo if e == 0 else ~lo
        return jnp.where(sel, qkv_ref[0, rows, off + j * LANES:off + (j + 1) * LANES], zero_b16)

    def phase_a(c, carry):
        rows = pl.ds(pl.multiple_of(c * L, L), L)
        G = og_ref[0, rows, GO:GO + LANES] + gbias
        LF = -_softplus(-G)
        pre = _cumsum_rows(tri_b16, LF)
        tot = pre[L - 1:L, :]
        Bm = jnp.where(bwd_lanes, tot - pre + LF, pre)
        U = pltpu.roll(G, H, 1) - Bm
        u_s[rows, :] = U
        b_s[rows, :] = Bm
        dec = tot + U
        mdec = jnp.max(dec, 0, keepdims=True)
        W = jnp.exp(dec - mdec)
        tot_s[c] = tot
        mdec_s[c] = mdec
        for j in range(H // 2):
            for e in range(2):
                h = 2 * j + e
                km = head_slices(rows, j, e, KO)
                vm = head_slices(rows, j, e, VO).astype(F32)
                parts = []
                for d in range(2):
                    wb = jnp.broadcast_to(W[:, _ML_FCOL(d, h):_ML_FCOL(d, h) + 1], (L, LANES))
                    parts += [wb * vm, wb]
                wv = jnp.concatenate(parts, axis=1).astype(BF16)
                ds_s[c * H + h] = _dot_tn(km, wv)
        return carry

    lax.fori_loop(0, nc, phase_a, 0, unroll=2)

    st_s[...] = jnp.zeros(st_s.shape, F32)

    def phase_b(i, m):
        cf, cb = i, nc - 1 - i
        totm = jnp.where(bwd_lanes1, tot_s[cb], tot_s[cf])
        mdecm = jnp.where(bwd_lanes1, mdec_s[cb], mdec_s[cf])
        mpf_s[cf] = m
        mpb_s[cb] = m
        m_new = jnp.maximum(totm + m, mdecm)
        w_c = jnp.exp(totm + m - m_new)
        w_d = jnp.exp(mdecm - m_new)
        for h in range(H):
            for d, c in ((0, cf), (1, cb)):
                part = slice(d * SW, (d + 1) * SW)
                col = slice(_ML_FCOL(d, h), _ML_FCOL(d, h) + 1)
                st = st_s[h, :, part]
                sp_s[c * H + h, :, part] = st.astype(BF16)
                st_s[h, :, part] = w_c[:, col] * st + w_d[:, col] * ds_s[c * H + h, :, part]
        return m_new

    lax.fori_loop(0, nc, phase_b, jnp.full((1, LANES), M_INIT, F32))

    ng = ng_ref[...]
    avg = avg_ref[...]

    def head_mean(t):
        hi = t.astype(BF16)
        lo_part = (t - hi.astype(F32)).astype(BF16)
        return _dot(jnp.concatenate([hi, lo_part], axis=1), avg)

    def phase_c(c, carry):
        rows = pl.ds(pl.multiple_of(c * L, L), L)
        U = u_s[rows, :]
        Bm = b_s[rows, :]
        UT = U.T
        m_prev = jnp.where(bwd_lanes1, mpb_s[c], mpf_s[c])
        Gm = Bm + m_prev
        Mt = jnp.maximum(Gm, Bm + _scan_max_rows(U, row, bwd_lanes))
        Z = Bm - Mt
        WI = jnp.exp(Gm - Mt)
        FL = jnp.exp(-Mt)
        for j in range(H // 2):
            pair = slice(j * LANES, (j + 1) * LANES)
            kb = qkv_ref[0, rows, KO + j * LANES:KO + (j + 1) * LANES]
            hsum = []
            for e in range(2):
                h = 2 * j + e
                qm = head_slices(rows, j, e, QO)
                vaug = jnp.concatenate([head_slices(rows, j, e, VO), ones_b16], axis=1)
                qk = _dot_nt(qm, kb)
                inter = _dot(qm, sp_s[c * H + h])
                ps = []
                for d in range(2):
                    fc = _ML_FCOL(d, h)
                    e_ts = jnp.where(tri[d], UT[fc:fc + 1, :], -jnp.inf) + Z[:, fc:fc + 1]
                    ps.append((qk * jnp.exp(e_ts)).astype(BF16))
                pv = _dot(jnp.concatenate(ps, axis=0), vaug)
                hs = None
                for d in range(2):
                    fc = _ML_FCOL(d, h)
                    nd = pv[d * L:(d + 1) * L] + WI[:, fc:fc + 1] * inter[:, d * SW:(d + 1) * SW]
                    hd = nd[:, 0:LANES] / jnp.maximum(jnp.abs(nd[:, LANES:SW]), FL[:, fc:fc + 1])
                    hs = hd if hs is None else hs + hd
                hsum.append(hs)
            hp = jnp.where(lo, hsum[0], hsum[1])
            dlt = hp - head_mean(hp)
            var = head_mean(dlt * dlt)
            hn = dlt * lax.rsqrt(var + 1e-5) * ng[:, pair]
            o = og_ref[0, rows, OO + j * LANES:OO + (j + 1) * LANES]
            y_ref[0, rows, pair] = (_sigmoid(o) * hn).astype(y_ref.dtype)
        return carry

    lax.fori_loop(0, nc, phase_c, 0, unroll=2)


def _mlstm(p_qkv, p_og, gate_bias, norm_g, avg):
    B, S, _ = p_qkv.shape
    nc = S // CHUNK
    H = MLSTM_HEADS
    vec = lambda: pltpu.VMEM((nc, 1, LANES), F32)
    return pl.pallas_call(
        _mlstm_kernel,
        grid=(B,),
        in_specs=[pl.BlockSpec((1, S, 3 * MLSTM_W), lambda b: (b, 0, 0)),
                  pl.BlockSpec((1, S, MLSTM_W + LANES), lambda b: (b, 0, 0)),
                  pl.BlockSpec((1, LANES), lambda b: (0, 0)),
                  pl.BlockSpec((1, MLSTM_W), lambda b: (0, 0)),
                  pl.BlockSpec(avg.shape, lambda b: (0, 0))],
        out_specs=pl.BlockSpec((1, S, MLSTM_W), lambda b: (b, 0, 0)),
        out_shape=jax.ShapeDtypeStruct((B, S, MLSTM_W), BF16),
        scratch_shapes=[pltpu.VMEM((S, LANES), F32), pltpu.VMEM((S, LANES), F32),
                        vec(), vec(), vec(), vec(),
                        pltpu.VMEM((nc * H, LANES, 4 * LANES), F32),
                        pltpu.VMEM((nc * H, LANES, 4 * LANES), BF16),
                        pltpu.VMEM((H, LANES, 4 * LANES), F32)],
        compiler_params=_cparams("parallel"),
        name="mlstm",
    )(p_qkv, p_og, gate_bias, norm_g, avg)


def _ssd_kernel(p_ref, cw_ref, cb_ref, dtb_ref, alog_ref, dsk_ref, ng_ref, y_ref,
                xpad_ref, xc_ref, cs_s, dt_s, tot_s, ds_s, sp_s, st_ref):
    S = p_ref.shape[1]
    L = CHUNK
    nc = S // L
    PAD = 8
    ZO, XO, DO = 0, SSD_INNER, SSD_INNER + SSD_CONV_DIM

    zpad = jnp.zeros((PAD, SSD_CONV_DIM), F32)
    xpad_ref[0:PAD, :] = zpad
    xpad_ref[PAD + S:PAD + S + PAD, :] = zpad
    R = 256
    for r in range(S // R):
        xpad_ref[PAD + r * R:PAD + (r + 1) * R, :] = p_ref[0, r * R:(r + 1) * R, XO:XO + SSD_CONV_DIM]
    half = SSD_CONV // 2
    for r in range(S // R):
        acc = jnp.zeros((R, SSD_CONV_DIM), F32) + cb_ref[...]
        for kk in range(SSD_CONV):
            off = PAD + r * R + kk - half
            acc = acc + xpad_ref[off:off + R, :] * cw_ref[kk:kk + 1, :]
        xc_ref[r * R:(r + 1) * R, :] = _silu(acc)

    ti = lax.broadcasted_iota(jnp.int32, (L, L), 0)
    si = lax.broadcasted_iota(jnp.int32, (L, L), 1)
    tri = (si <= ti, si >= ti)
    tri_b16 = tri[0].astype(BF16)
    lane = lax.broadcasted_iota(jnp.int32, (L, LANES), 1)
    lane1 = lax.broadcasted_iota(jnp.int32, (1, LANES), 1)
    lo = lane < SSD_HEADDIM
    lo1 = lane1 < SSD_HEADDIM
    bwd_lanes = lane >= SSD_HEADS
    a_neg = -jnp.exp(alog_ref[...])
    dtb = dtb_ref[...]
    BO, CO = SSD_INNER, SSD_INNER + SSD_GROUPS * SSD_STATE
    NG = SSD_GROUPS
    col_of = lambda d, g, e: d * SSD_HEADS + NG * g + e

    def pair_cols(a, d, g, mask):
        ce = col_of(d, g, 0)
        return jnp.where(mask, a[:, ce:ce + 1], a[:, ce + 1:ce + 2])

    def phase_a(c, carry):
        rows = pl.ds(pl.multiple_of(c * L, L), L)
        dt = _softplus(p_ref[0, rows, DO:DO + LANES] + dtb)
        dta = dt * a_neg
        pre = _cumsum_rows(tri_b16, dta)
        tot = pre[L - 1:L, :]
        cs = jnp.where(bwd_lanes, tot - pre + dta, pre)
        cs_s[rows, :] = cs
        dt_s[rows, :] = dt
        dec = jnp.exp(tot - cs) * dt
        for g in range(NG):
            xp = xc_ref[rows, g * LANES:(g + 1) * LANES]
            Bg = xc_ref[rows, BO + g * LANES:BO + (g + 1) * LANES].astype(BF16)
            xdt = jnp.concatenate([xp * pair_cols(dec, d, g, lo) for d in range(2)], axis=1)
            inc = _dot_tn(Bg, xdt.astype(BF16))
            for d in range(2):
                tot_s[c * 2 * NG + d * NG + g] = pair_cols(tot, d, g, lo1)
                ds_s[c * 2 * NG + d * NG + g] = inc[:, d * LANES:(d + 1) * LANES]
        return carry

    lax.fori_loop(0, nc, phase_a, 0, unroll=4)

    st_ref[...] = jnp.zeros(st_ref.shape, F32)

    def phase_b(i, carry):
        for d in range(2):
            c = i if d == 0 else nc - 1 - i
            for g in range(NG):
                k = d * NG + g
                st = st_ref[k]
                sp_s[c * NG + g, :, d * LANES:(d + 1) * LANES] = st.astype(BF16)
                st_ref[k] = jnp.exp(tot_s[c * 2 * NG + k]) * st + ds_s[c * 2 * NG + k]
        return carry

    lax.fori_loop(0, nc, phase_b, 0)

    dsk = dsk_ref[...]
    ng = ng_ref[...]

    def phase_c(c, carry):
        rows = pl.ds(pl.multiple_of(c * L, L), L)
        cs = cs_s[rows, :]
        cst = cs.T
        dtt = dt_s[rows, :].T
        for g in range(NG):
            pair = slice(g * LANES, (g + 1) * LANES)
            xp = xc_ref[rows, pair]
            xb = xp.astype(BF16)
            Bg = xc_ref[rows, BO + g * LANES:BO + (g + 1) * LANES].astype(BF16)
            Cg = xc_ref[rows, CO + g * LANES:CO + (g + 1) * LANES].astype(BF16)
            CB = _dot_nt(Cg, Bg)
            yi = _dot(Cg, sp_s[c * NG + g])
            bc = [[jnp.broadcast_to(cs[:, col_of(d, g, e):col_of(d, g, e) + 1], (L, LANES))
                   for e in range(2)] for d in range(2)]
            ys = []
            for e in range(2):
                W = None
                for d in range(2):
                    col = col_of(d, g, e)
                    arg = jnp.where(tri[d], bc[d][e] - cst[col:col + 1, :], -jnp.inf)
                    w = jnp.exp(arg) * dtt[col:col + 1, :]
                    W = w if W is None else W + w
                ys.append(_dot((CB * W).astype(BF16), xb))
            y = jnp.where(lo, ys[0], ys[1])
            for d in range(2):
                y = y + yi[:, d * LANES:(d + 1) * LANES] * jnp.exp(jnp.where(lo, bc[d][0], bc[d][1]))
            y = y + xp * dsk[:, pair]
            y = y * _silu(p_ref[0, rows, ZO + g * LANES:ZO + (g + 1) * LANES])
            y = y * lax.rsqrt(jnp.mean(y * y, -1, keepdims=True) + 1e-6) * ng[:, pair]
            y_ref[0, rows, pair] = y.astype(y_ref.dtype)
        return carry

    lax.fori_loop(0, nc, phase_c, 0, unroll=2)


def _ssd(p_ssd, conv_w, conv_b, dt_bias, a_log, d_skip, norm_g):
    B, S, _ = p_ssd.shape
    nc = S // CHUNK
    NG = SSD_GROUPS
    full = lambda a: pl.BlockSpec(a.shape, lambda b: (0, 0))
    return pl.pallas_call(
        _ssd_kernel,
        grid=(B,),
        in_specs=[pl.BlockSpec((1, S, P_SSD_W), lambda b: (b, 0, 0)),
                  full(conv_w), full(conv_b), full(dt_bias), full(a_log), full(d_skip), full(norm_g)],
        out_specs=pl.BlockSpec((1, S, SSD_INNER), lambda b: (b, 0, 0)),
        out_shape=jax.ShapeDtypeStruct((B, S, SSD_INNER), BF16),
        scratch_shapes=[pltpu.VMEM((S + 16, SSD_CONV_DIM), F32), pltpu.VMEM((S, SSD_CONV_DIM), F32),
                        pltpu.VMEM((S, LANES), F32), pltpu.VMEM((S, LANES), F32),
                        pltpu.VMEM((nc * 2 * NG, 1, LANES), F32),
                        pltpu.VMEM((nc * 2 * NG, SSD_STATE, LANES), F32),
                        pltpu.VMEM((nc * NG, SSD_STATE, 2 * LANES), BF16),
                        pltpu.VMEM((2 * NG, SSD_STATE, LANES), F32)],
        compiler_params=_cparams("parallel"),
        name="ssd",
    )(p_ssd, conv_w, conv_b, dt_bias, a_log, d_skip, norm_g)


def _layernorm(u, g, b):
    mu = jnp.mean(u, -1, keepdims=True)
    d = u - mu
    var = jnp.mean(d * d, -1, keepdims=True)
    return d * lax.rsqrt(var + 1e-5) * g + b


def _route(logits):
    lane_i = lax.broadcasted_iota(jnp.int32, logits.shape, 1)
    lane = lane_i.astype(F32)
    group_of_lane = jnp.right_shift(lane_i - N_GROUPS, 2).astype(F32)
    neg = -jnp.inf
    big = 1e6
    glm = jnp.where(lane_i < N_GROUPS, logits, neg)
    gmax = jnp.max(glm, -1, keepdims=True)
    gp = 1.0 / jnp.sum(jnp.exp(glm - gmax), -1, keepdims=True)
    gi = jnp.min(jnp.where(glm == gmax, lane, big), -1, keepdims=True)
    in_group = (lane_i >= N_GROUPS) & (lane_i < N_GROUPS + N_EXPERTS) & (group_of_lane == gi)
    elm = jnp.where(in_group, logits, neg)
    e1 = jnp.max(elm, -1, keepdims=True)
    i1 = jnp.min(jnp.where(elm == e1, lane, big), -1, keepdims=True)
    elm2 = jnp.where(lane == i1, neg, elm)
    e2 = jnp.max(elm2, -1, keepdims=True)
    i2 = jnp.min(jnp.where(elm2 == e2, lane, big), -1, keepdims=True)
    r = jnp.exp(e2 - e1)
    p1 = 1.0 / (1.0 + r)
    p2 = r / (1.0 + r)
    return jnp.where(lane == i1, gp * p1, jnp.where(lane == i2, gp * p2, 0.0))


OUT_SUB = 256


def _outproj_kernel(alpha, x_ref, ya_ref, yb_ref, yc_ref, wo_ref, g_ref, b_ref, wr_ref, br_ref,
                    x1_ref, cmb_ref):
    tm = x_ref.shape[0]
    for r in range(tm // OUT_SUB):
        rows = slice(r * OUT_SUB, (r + 1) * OUT_SUB)
        mix = (_dot(ya_ref[rows, :], wo_ref[0:MLA_OUT, :])
               + _dot(yb_ref[rows, :], wo_ref[MLA_OUT:MLA_OUT + MLSTM_W, :])
               + _dot(yc_ref[rows, :], wo_ref[MLA_OUT + MLSTM_W:, :]))
        x1 = _layernorm(alpha * x_ref[rows, :] + mix, g_ref[...], b_ref[...])
        x1_ref[rows, :] = x1
        x_hi = x1.astype(BF16)
        x_lo = (x1 - x_hi.astype(F32)).astype(BF16)
        a = _dot(x_hi, wr_ref[...])
        c = _dot(x_lo, wr_ref[...])
        logits = (a[:, 0:LANES] + (a[:, LANES:] + c[:, 0:LANES]) + c[:, LANES:]) + br_ref[...]
        cmb_ref[rows, :] = _route(logits)


def _outproj(alpha, x2, ya, yb, yc, wo, g, b, wr, br, tm=1024):
    T, D = x2.shape
    row = lambda w: pl.BlockSpec((tm, w), lambda i: (i, 0))
    full = lambda a: pl.BlockSpec(a.shape, lambda i: (0, 0))
    return pl.pallas_call(
        functools.partial(_outproj_kernel, alpha),
        grid=(T // tm,),
        in_specs=[row(D), row(MLA_OUT), row(MLSTM_W), row(SSD_INNER), full(wo), full(g), full(b),
                  full(wr), full(br)],
        out_specs=[row(D), row(LANES)],
        out_shape=[jax.ShapeDtypeStruct((T, D), F32), jax.ShapeDtypeStruct((T, LANES), F32)],
        compiler_params=_cparams("parallel"),
        name="outproj_ln_router",
    )(x2, ya, yb, yc, wo, g, b, wr, br)


def _moe_kernel(alpha, x_ref, cmb_ref, wg_ref, wu_ref, wd_ref, g_ref, b_ref, o_ref, xb_ref, acc_ref):
    grp = pl.program_id(1)

    @pl.when(grp == 0)
    def _():
        xb_ref[...] = x_ref[...].astype(BF16)
        acc_ref[...] = jnp.zeros(acc_ref.shape, F32)

    xb = xb_ref[...]
    cmb = cmb_ref[...]
    lane = lax.broadcasted_iota(jnp.int32, cmb.shape, 1)
    for e in range(EXPERTS_PER_GROUP):
        col = N_GROUPS + grp * EXPERTS_PER_GROUP + e
        c = jnp.sum(jnp.where(lane == col, cmb, 0.0), -1, keepdims=True)
        h = _silu(_dot(xb, wg_ref[e])) * _dot(xb, wu_ref[e]) * c
        acc_ref[...] += _dot(h.astype(BF16), wd_ref[e])

    @pl.when(grp == N_GROUPS - 1)
    def _():
        o_ref[...] = _layernorm(alpha * x_ref[...] + acc_ref[...], g_ref[...], b_ref[...])


def _moe(alpha, x1, cmb, wg, wu, wd, g, b, tm=1024):
    T, D = x1.shape
    F = wg.shape[-1]
    E = EXPERTS_PER_GROUP
    return pl.pallas_call(
        functools.partial(_moe_kernel, alpha),
        grid=(T // tm, N_GROUPS),
        in_specs=[pl.BlockSpec((tm, D), lambda i, e: (i, 0)),
                  pl.BlockSpec((tm, LANES), lambda i, e: (i, 0)),
                  pl.BlockSpec((E, D, F), lambda i, e: (e, 0, 0)),
                  pl.BlockSpec((E, D, F), lambda i, e: (e, 0, 0)),
                  pl.BlockSpec((E, F, D), lambda i, e: (e, 0, 0)),
                  pl.BlockSpec((1, D), lambda i, e: (0, 0)),
                  pl.BlockSpec((1, D), lambda i, e: (0, 0))],
        out_specs=pl.BlockSpec((tm, D), lambda i, e: (i, 0)),
        out_shape=jax.ShapeDtypeStruct((T, D), F32),
        scratch_shapes=[pltpu.VMEM((tm, D), BF16), pltpu.VMEM((tm, D), F32)],
        compiler_params=_cparams("parallel", "arbitrary"),
        name="moe_ln",
    )(x1, cmb, wg, wu, wd, g, b)


def _pad_lanes(a, width=LANES):
    return jnp.pad(a, [(0, 0)] * (a.ndim - 1) + [(0, width - a.shape[-1])])


def kernel(x, positions, w_in, mla_q_norm, mla_kv_norm, mla_w_uq, mla_w_ukv, mlstm_gate_bias, mlstm_norm,
           ssd_conv_w, ssd_conv_b, ssd_dt_bias, ssd_a_log, ssd_d, ssd_norm, w_out, ln1_g, ln1_b,
           router_group_w, router_group_b, router_expert_w, router_expert_b,
           expert_w_gate, expert_w_up, expert_w_down, ln2_g, ln2_b):
    B, S, D = x.shape
    depth = w_in.shape[0]
    T = B * S
    alpha = (2 * depth) ** 0.25

    inv = ROPE_BASE ** (-jnp.arange(0, MLA_ROPE, 2, dtype=F32) / MLA_ROPE)
    ang = positions.astype(F32).reshape(T, 1) * inv
    cos, sin = jnp.cos(ang), jnp.sin(ang)
    ones = jnp.ones((T, MLA_NOPE), F32)
    zeros = jnp.zeros((T, MLA_NOPE), F32)
    pad = jnp.zeros((T, LANES - MLA_NOPE - MLA_ROPE), F32)
    cos_t = jnp.concatenate([ones, cos, cos, pad], -1)
    sin_t = jnp.concatenate([zeros, -sin, sin, pad], -1)

    o = [0]
    for s in (MLA_Q_RANK, MLA_KV_RANK, MLA_ROPE, MLSTM_W, MLSTM_W, MLSTM_W, MLSTM_W, 4 * MLSTM_HEADS,
              SSD_INNER, SSD_CONV_DIM, 2 * SSD_HEADS):
        o.append(o[-1] + s)
    zc = lambda n: jnp.zeros((depth, D, n), w_in.dtype)
    w_a = jnp.concatenate([w_in[..., o[0]:o[2]], zc(MLA_NOPE), w_in[..., o[2]:o[3]],
                           zc(LANES - MLA_NOPE - MLA_ROPE)], -1).astype(BF16)
    w_qkv = jnp.concatenate([w_in[..., o[3]:o[4]], w_in[..., o[4]:o[5]] * (MLSTM_DH ** -0.5),
                             w_in[..., o[5]:o[6]]], -1).astype(BF16)
    w_og = jnp.concatenate([w_in[..., o[6]:o[8]], zc(LANES - 4 * MLSTM_HEADS)], -1).astype(BF16)
    head_of_lane = jnp.arange(LANES) // MLSTM_DH
    avg_blk = (head_of_lane[:, None] == head_of_lane[None, :]).astype(BF16) * (1.0 / MLSTM_DH)
    ml_avg = jnp.concatenate([avg_blk, avg_blk], 0)
    w_c = jnp.concatenate([w_in[..., o[8]:o[11]], zc(LANES - 2 * SSD_HEADS)], -1).astype(BF16)

    wq = mla_w_uq.reshape(depth, MLA_Q_RANK, MLA_HEADS, MLA_NOPE + MLA_ROPE)
    wq = _pad_lanes(wq).reshape(depth, MLA_Q_RANK, MLA_HEADS * LANES).astype(BF16)
    wkv = mla_w_ukv.reshape(depth, MLA_KV_RANK, MLA_HEADS, MLA_NOPE + MLA_V)
    wk = _pad_lanes(wkv[..., :MLA_NOPE]).reshape(depth, MLA_KV_RANK, MLA_HEADS * LANES).astype(BF16)
    wv = wkv[..., MLA_NOPE:].reshape(depth, MLA_KV_RANK, MLA_HEADS // 2, 2 * MLA_V)
    wv = _pad_lanes(wv, 2 * LANES).reshape(depth, MLA_KV_RANK, MLA_HEADS * LANES).astype(BF16)

    gate_bias = _pad_lanes(mlstm_gate_bias.reshape(depth, 1, 4 * MLSTM_HEADS))
    dt_bias = _pad_lanes(ssd_dt_bias.reshape(depth, 1, 2 * SSD_HEADS))
    a_log = _pad_lanes(ssd_a_log.reshape(depth, 1, 2 * SSD_HEADS))
    d_skip = jnp.repeat(ssd_d, SSD_HEADDIM, axis=-1).reshape(depth, 1, SSD_INNER)
    conv_w = jnp.pad(ssd_conv_w, ((0, 0), (0, 8 - SSD_CONV), (0, 0)))

    w_o = w_out.astype(BF16)
    w_r = _pad_lanes(jnp.concatenate(
        [router_group_w, router_expert_w.transpose(0, 2, 1, 3).reshape(depth, D, N_EXPERTS)], -1))
    w_r_hi = w_r.astype(BF16)
    w_r2 = jnp.concatenate([w_r_hi, (w_r - w_r_hi.astype(F32)).astype(BF16)], -1)
    b_r = _pad_lanes(jnp.concatenate(
        [router_group_b, router_expert_b.reshape(depth, N_EXPERTS)], -1).reshape(depth, 1, -1))
    e_g = expert_w_gate.astype(BF16)
    e_u = expert_w_up.astype(BF16)
    e_d = expert_w_down.astype(BF16)
    row = lambda a, l: a[l].reshape(1, -1)

    x2 = x.reshape(T, D)
    for l in range(depth):
        p_mla, p_qkv, p_og, p_ssd = _inproj(x2, w_a[l], w_qkv[l], w_og[l], w_c[l])
        q, k, v = _mla_prep(p_mla, cos_t, sin_t, row(mla_q_norm, l), row(mla_kv_norm, l), wq[l], wk[l], wv[l])
        y_a = _attention(q.reshape(B, S, -1), k.reshape(B, S, -1), v.reshape(B, S, -1))
        y_b = _mlstm(p_qkv.reshape(B, S, -1), p_og.reshape(B, S, -1), gate_bias[l], row(mlstm_norm, l), ml_avg)
        y_c = _ssd(p_ssd.reshape(B, S, -1), conv_w[l], row(ssd_conv_b, l), dt_bias[l], a_log[l],
                   d_skip[l], row(ssd_norm, l))
        x1, cmb = _outproj(alpha, x2, y_a.reshape(T, -1), y_b.reshape(T, -1), y_c.reshape(T, -1),
                           w_o[l], row(ln1_g, l), row(ln1_b, l), w_r2[l], b_r[l])
        x2 = _moe(alpha, x1, cmb, e_g[l], e_u[l], e_d[l], row(ln2_g, l), row(ln2_b, l))
    return x2.reshape(B, S, D)
```

```python
import functools

import jax
import jax.numpy as jnp
from jax import lax
from jax.experimental import pallas as pl
from jax.experimental.pallas import tpu as pltpu

F32 = jnp.float32
BF16 = jnp.bfloat16
HIGHEST = lax.Precision.HIGHEST

LANES = 128
VMEM_LIMIT = 56 * 1024 * 1024

MLA_HEADS = 8
MLA_NOPE = 64
MLA_ROPE = 32
MLA_V = 64
MLA_Q_RANK = 256
MLA_KV_RANK = 128
ROPE_BASE = 10000.0
MLSTM_HEADS = 4
MLSTM_DH = 64
SSD_HEADS = 4
SSD_HEADDIM = 64
SSD_GROUPS = 2
SSD_STATE = 128
SSD_CONV = 5
N_GROUPS = 4
EXPERTS_PER_GROUP = 4
N_EXPERTS = 16
M_INIT = -1e30
LOG2E = 1.4426950408889634

MLA_OUT = MLA_HEADS * MLA_V
MLSTM_W = MLSTM_HEADS * MLSTM_DH
SSD_INNER = SSD_HEADS * SSD_HEADDIM
SSD_CONV_DIM = SSD_INNER + 2 * SSD_GROUPS * SSD_STATE
CHUNK = 128

P_MLA_W = 512
P_ML_W = 4 * MLSTM_W + LANES
P_SSD_W = SSD_INNER + SSD_CONV_DIM + LANES


def _cparams(*sem):
    return pltpu.CompilerParams(dimension_semantics=sem, vmem_limit_bytes=VMEM_LIMIT)


def _dot(a, b):
    return jnp.dot(a, b, preferred_element_type=F32)


def _dot_nt(a, b):
    return lax.dot_general(a, b, (((1,), (1,)), ((), ())), preferred_element_type=F32)


def _dot_tn(a, b):
    return lax.dot_general(a, b, (((0,), (0,)), ((), ())), preferred_element_type=F32)


def _dot_f32(a, b):
    return jnp.dot(a, b, preferred_element_type=F32, precision=HIGHEST)


def _split3(x):
    hi = x.astype(BF16)
    r = x - hi.astype(F32)
    mid = r.astype(BF16)
    lo = (r - mid.astype(F32)).astype(BF16)
    return hi, mid, lo


def _cumsum_rows(tri_b16, x):
    hi, mid, lo = _split3(x)
    c = _dot(tri_b16, jnp.concatenate([hi, mid, lo], axis=1))
    return c[:, 2 * LANES:3 * LANES] + c[:, LANES:2 * LANES] + c[:, 0:LANES]


def _softplus(x):
    return jnp.maximum(x, 0.0) + jnp.log1p(jnp.exp(-jnp.abs(x)))


def _sigmoid(x):
    return 1.0 / (1.0 + jnp.exp(-x))


def _silu(x):
    return x * _sigmoid(x)


IN_COLS = (0, P_MLA_W, P_MLA_W + 3 * MLSTM_W, P_MLA_W + 4 * MLSTM_W + LANES,
           P_MLA_W + 4 * MLSTM_W + LANES + P_SSD_W)


def _inproj_kernel(x_ref, w_ref, cos_ref, sin_ref, qn_ref, kvn_ref,
                   wq_ref, wk_ref, wv_ref, q_ref, k_ref, v_ref, pq_ref, pg_ref, pc_ref):
    x = x_ref[...].astype(BF16)
    c = IN_COLS
    pq_ref[...] = _dot(x, w_ref[:, c[1]:c[2]]).astype(pq_ref.dtype)
    pg_ref[...] = _dot(x, w_ref[:, c[2]:c[3]])
    pc_ref[...] = _dot(x, w_ref[:, c[3]:c[4]])
    p = _dot(x, w_ref[:, c[0]:c[1]])
    cq = p[:, 0:MLA_Q_RANK]
    ckv = p[:, MLA_Q_RANK:MLA_Q_RANK + MLA_KV_RANK]
    krb = p[:, MLA_Q_RANK + MLA_KV_RANK:P_MLA_W]
    cqn = (cq * lax.rsqrt(jnp.mean(cq * cq, -1, keepdims=True) + 1e-6) * qn_ref[...]).astype(BF16)
    ckn = (ckv * lax.rsqrt(jnp.mean(ckv * ckv, -1, keepdims=True) + 1e-6) * kvn_ref[...]).astype(BF16)
    q = _dot(cqn, wq_ref[...])
    k = _dot(ckn, wk_ref[...])
    v = _dot(ckn, wv_ref[...])
    vlane = lax.broadcasted_iota(jnp.int32, v.shape, 1)
    v_ref[...] = jnp.where(vlane % (2 * LANES) >= LANES, 1.0, v).astype(v_ref.dtype)
    cos = cos_ref[...]
    sin = sin_ref[...]
    lane = lax.broadcasted_iota(jnp.int32, cos.shape, 1)
    first_half = lane < MLA_NOPE + MLA_ROPE // 2

    def rope(t):
        partner = jnp.where(first_half, pltpu.roll(t, LANES - MLA_ROPE // 2, 1),
                            pltpu.roll(t, MLA_ROPE // 2, 1))
        return t * cos + partner * sin

    scale = (MLA_NOPE + MLA_ROPE) ** -0.5 * LOG2E
    kr = rope(krb)
    for h in range(MLA_HEADS):
        sl = slice(h * LANES, (h + 1) * LANES)
        q_ref[:, sl] = (rope(q[:, sl]) * scale).astype(q_ref.dtype)
        k_ref[:, sl] = (k[:, sl] + kr).astype(k_ref.dtype)


def _inproj(x2, w_all, cos, sin, qn, kvn, wq, wk, wv, tm=512):
    T, D = x2.shape
    row = lambda w: pl.BlockSpec((tm, w), lambda i: (i, 0))
    full = lambda a: pl.BlockSpec(a.shape, lambda i: (0, 0))
    HW = MLA_HEADS * LANES
    return pl.pallas_call(
        _inproj_kernel,
        grid=(T // tm,),
        in_specs=[row(D), full(w_all), row(LANES), row(LANES),
                  full(qn), full(kvn), full(wq), full(wk), full(wv)],
        out_specs=[row(HW), row(HW), row(HW), row(3 * MLSTM_W), row(MLSTM_W + LANES), row(P_SSD_W)],
        out_shape=[jax.ShapeDtypeStruct((T, HW), BF16)] * 3 + [
            jax.ShapeDtypeStruct((T, 3 * MLSTM_W), BF16),
            jax.ShapeDtypeStruct((T, MLSTM_W + LANES), F32),
            jax.ShapeDtypeStruct((T, P_SSD_W), F32)],
        compiler_params=_cparams("parallel"),
        name="inproj_mla_prep",
    )(x2, w_all, cos, sin, qn, kvn, wq, wk, wv)


ATTN_SUB = 128


def _attn_kernel(q_ref, k_ref, v_ref, o_ref):
    tq = q_ref.shape[1]
    lane = lax.broadcasted_iota(jnp.int32, (ATTN_SUB, LANES), 1)
    for r in range(tq // ATTN_SUB):
        rows = slice(r * ATTN_SUB, (r + 1) * ATTN_SUB)
        outs = []
        for hh in range(2):
            sl = slice(hh * LANES, (hh + 1) * LANES)
            s = _dot_nt(q_ref[0, rows, sl], k_ref[0, :, sl])
            p = jnp.exp2(s - jnp.max(s, -1, keepdims=True)).astype(BF16)
            ov = _dot(p, v_ref[0])
            outs.append(ov[:, 0:LANES] / ov[:, LANES:])
        o_ref[0, rows, :] = jnp.where(lane < MLA_V, outs[0], outs[1]).astype(o_ref.dtype)


def _attention(q, k, v, tq=512):
    B, S, _ = q.shape
    tq = min(tq, S)
    return pl.pallas_call(
        _attn_kernel,
        grid=(B, MLA_HEADS // 2, S // tq),
        in_specs=[pl.BlockSpec((1, tq, 2 * LANES), lambda b, j, i: (b, i, j)),
                  pl.BlockSpec((1, S, 2 * LANES), lambda b, j, i: (b, 0, j)),
                  pl.BlockSpec((1, S, 2 * LANES), lambda b, j, i: (b, 0, j))],
        out_specs=pl.BlockSpec((1, tq, LANES), lambda b, j, i: (b, i, j)),
        out_shape=jax.ShapeDtypeStruct((B, S, MLA_OUT), BF16),
        compiler_params=_cparams("parallel", "parallel", "arbitrary"),
        name="mla_attn",
    )(q, k, v)


_ML_FCOL = lambda d, h: 8 * d + MLSTM_HEADS + h
_ML_ICOL = lambda d, h: 8 * d + h


def _scan_max_rows(x, row, bwd_lanes):
    L = x.shape[0]
    neg = -jnp.inf
    xf, xb = x, x
    k = 1
    while k < L:
        xf = jnp.maximum(xf, jnp.where(row >= k, pltpu.roll(xf, k, 0), neg))
        xb = jnp.maximum(xb, jnp.where(row < L - k, pltpu.roll(xb, L - k, 0), neg))
        k *= 2
    return jnp.where(bwd_lanes, xb, xf)


def _mlstm_kernel(qkv_ref, og_ref, gb_ref, ng_ref, avg_ref, y_ref,
                  u_s, b_s, tot_s, mdec_s, mpf_s, mpb_s, ds_s, sp_s, st_s):
    S = qkv_ref.shape[1]
    L = CHUNK
    nc = S // L
    H = MLSTM_HEADS
    SW = 2 * LANES
    ti = lax.broadcasted_iota(jnp.int32, (L, L), 0)
    si = lax.broadcasted_iota(jnp.int32, (L, L), 1)
    tri = (si <= ti, si >= ti)
    tri_b16 = tri[0].astype(BF16)
    lane = lax.broadcasted_iota(jnp.int32, (L, LANES), 1)
    row = lax.broadcasted_iota(jnp.int32, (L, LANES), 0)
    lane1 = lax.broadcasted_iota(jnp.int32, (1, LANES), 1)
    bwd_lanes = lane >= 2 * H
    bwd_lanes1 = lane1 >= 2 * H
    lo = lane < MLSTM_DH
    gbias = gb_ref[...]
    ones_b16 = jnp.ones((L, LANES), BF16)
    zero_b16 = jnp.zeros((L, LANES), BF16)
    QO, KO, VO = 0, MLSTM_W, 2 * MLSTM_W
    OO, GO = 0, MLSTM_W

    def head_slices(rows, j, e, off, fill=zero_b16):
        sel = lo if e == 0 else ~lo
        return jnp.where(sel, qkv_ref[0, rows, off + j * LANES:off + (j + 1) * LANES], fill)

    def phase_a(c, carry):
        rows = pl.ds(pl.multiple_of(c * L, L), L)
        G = og_ref[0, rows, GO:GO + LANES] + gbias
        LF = -_softplus(-G)
        pre = _cumsum_rows(tri_b16, LF)
        tot = pre[L - 1:L, :]
        Bm = jnp.where(bwd_lanes, tot - pre + LF, pre)
        U = pltpu.roll(G, H, 1) - Bm
        u_s[rows, :] = U
        b_s[rows, :] = Bm
        dec = tot + U
        mdec = jnp.max(dec, 0, keepdims=True)
        W = jnp.exp(dec - mdec)
        tot_s[c] = tot
        mdec_s[c] = mdec
        for j in range(H // 2):
            for e in range(2):
                h = 2 * j + e
                km = head_slices(rows, j, e, KO)
                vm = head_slices(rows, j, e, VO).astype(F32)
                parts = []
                for d in range(2):
                    wb = jnp.broadcast_to(W[:, _ML_FCOL(d, h):_ML_FCOL(d, h) + 1], (L, LANES))
                    parts += [wb * vm, wb]
                wv = jnp.concatenate(parts, axis=1).astype(BF16)
                ds_s[c * H + h] = _dot_tn(km, wv)
        return carry

    lax.fori_loop(0, nc, phase_a, 0, unroll=2)

    st_s[...] = jnp.zeros(st_s.shape, F32)

    def phase_b(i, m):
        cf, cb = i, nc - 1 - i
        totm = jnp.where(bwd_lanes1, tot_s[cb], tot_s[cf])
        mdecm = jnp.where(bwd_lanes1, mdec_s[cb], mdec_s[cf])
        mpf_s[cf] = m
        mpb_s[cb] = m
        m_new = jnp.maximum(totm + m, mdecm)
        w_c = jnp.exp(totm + m - m_new)
        w_d = jnp.exp(mdecm - m_new)
        for h in range(H):
            for d, c in ((0, cf), (1, cb)):
                part = slice(d * SW, (d + 1) * SW)
                col = slice(_ML_FCOL(d, h), _ML_FCOL(d, h) + 1)
                st = st_s[h, :, part]
                sp_s[c * H + h, :, part] = st.astype(BF16)
                st_s[h, :, part] = w_c[:, col] * st + w_d[:, col] * ds_s[c * H + h, :, part]
        return m_new

    lax.fori_loop(0, nc, phase_b, jnp.full((1, LANES), M_INIT, F32))

    ng = ng_ref[...]
    avg = avg_ref[...]

    def head_mean(t):
        hi = t.astype(BF16)
        lo_part = (t - hi.astype(F32)).astype(BF16)
        return _dot(jnp.concatenate([hi, lo_part], axis=1), avg)

    def phase_c(c, carry):
        rows = pl.ds(pl.multiple_of(c * L, L), L)
        U = u_s[rows, :]
        Bm = b_s[rows, :]
        UT = U.T
        m_prev = jnp.where(bwd_lanes1, mpb_s[c], mpf_s[c])
        Gm = Bm + m_prev
        Mt = jnp.maximum(Gm, Bm + _scan_max_rows(U, row, bwd_lanes))
        Z = Bm - Mt
        WI = jnp.exp(Gm - Mt)
        FL = jnp.exp(-Mt)
        for j in range(H // 2):
            pair = slice(j * LANES, (j + 1) * LANES)
            kb = qkv_ref[0, rows, KO + j * LANES:KO + (j + 1) * LANES]
            hsum = []
            for e in range(2):
                h = 2 * j + e
                qm = head_slices(rows, j, e, QO)
                vaug = jnp.concatenate([head_slices(rows, j, e, VO), ones_b16], axis=1)
                qk = _dot_nt(qm, kb)
                inter = _dot(qm, sp_s[c * H + h])
                ps = []
                for d in range(2):
                    fc = _ML_FCOL(d, h)
                    e_ts = jnp.where(tri[d], UT[fc:fc + 1, :], -jnp.inf) + Z[:, fc:fc + 1]
                    ps.append((qk * jnp.exp(e_ts)).astype(BF16))
                pv = _dot(jnp.concatenate(ps, axis=0), vaug)
                hs = None
                for d in range(2):
                    fc = _ML_FCOL(d, h)
                    nd = pv[d * L:(d + 1) * L] + WI[:, fc:fc + 1] * inter[:, d * SW:(d + 1) * SW]
                    hd = nd[:, 0:LANES] / jnp.maximum(jnp.abs(nd[:, LANES:SW]), FL[:, fc:fc + 1])
                    hs = hd if hs is None else hs + hd
                hsum.append(hs)
            hp = jnp.where(lo, hsum[0], hsum[1])
            dlt = hp - head_mean(hp)
            var = head_mean(dlt * dlt)
            hn = dlt * lax.rsqrt(var + 1e-5) * ng[:, pair]
            o = og_ref[0, rows, OO + j * LANES:OO + (j + 1) * LANES]
            y_ref[0, rows, pair] = (_sigmoid(o) * hn).astype(y_ref.dtype)
        return carry

    lax.fori_loop(0, nc, phase_c, 0, unroll=2)


def _mlstm(p_qkv, p_og, gate_bias, norm_g, avg):
    B, S, _ = p_qkv.shape
    nc = S // CHUNK
    H = MLSTM_HEADS
    vec = lambda: pltpu.VMEM((nc, 1, LANES), F32)
    return pl.pallas_call(
        _mlstm_kernel,
        grid=(B,),
        in_specs=[pl.BlockSpec((1, S, 3 * MLSTM_W), lambda b: (b, 0, 0)),
                  pl.BlockSpec((1, S, MLSTM_W + LANES), lambda b: (b, 0, 0)),
                  pl.BlockSpec((1, LANES), lambda b: (0, 0)),
                  pl.BlockSpec((1, MLSTM_W), lambda b: (0, 0)),
                  pl.BlockSpec(avg.shape, lambda b: (0, 0))],
        out_specs=pl.BlockSpec((1, S, MLSTM_W), lambda b: (b, 0, 0)),
        out_shape=jax.ShapeDtypeStruct((B, S, MLSTM_W), BF16),
        scratch_shapes=[pltpu.VMEM((S, LANES), F32), pltpu.VMEM((S, LANES), F32),
                        vec(), vec(), vec(), vec(),
                        pltpu.VMEM((nc * H, LANES, 4 * LANES), F32),
                        pltpu.VMEM((nc * H, LANES, 4 * LANES), BF16),
                        pltpu.VMEM((H, LANES, 4 * LANES), F32)],
        compiler_params=_cparams("parallel"),
        name="mlstm",
    )(p_qkv, p_og, gate_bias, norm_g, avg)


def _ssd_kernel(p_ref, cw_ref, cb_ref, dtb_ref, alog_ref, dsk_ref, ng_ref, y_ref,
                xpad_ref, xc_ref, cs_s, dt_s, tot_s, ds_s, sp_s, st_ref):
    S = p_ref.shape[1]
    L = CHUNK
    nc = S // L
    PAD = 8
    ZO, XO, DO = 0, SSD_INNER, SSD_INNER + SSD_CONV_DIM

    zpad = jnp.zeros((PAD, SSD_CONV_DIM), F32)
    xpad_ref[0:PAD, :] = zpad
    xpad_ref[PAD + S:PAD + S + PAD, :] = zpad
    R = 256
    for r in range(S // R):
        xpad_ref[PAD + r * R:PAD + (r + 1) * R, :] = p_ref[0, r * R:(r + 1) * R, XO:XO + SSD_CONV_DIM]
    half = SSD_CONV // 2
    for r in range(S // R):
        win = xpad_ref[r * R:r * R + R + 2 * PAD, :]
        acc = jnp.zeros((R, SSD_CONV_DIM), F32) + cb_ref[...]
        for kk in range(SSD_CONV):
            sh = (half - kk) % (R + 2 * PAD)
            tap = win if sh == 0 else pltpu.roll(win, sh, 0)
            acc = acc + tap[PAD:PAD + R, :] * cw_ref[kk:kk + 1, :]
        xc_ref[r * R:(r + 1) * R, :] = _silu(acc)

    ti = lax.broadcasted_iota(jnp.int32, (L, L), 0)
    si = lax.broadcasted_iota(jnp.int32, (L, L), 1)
    tri = (si <= ti, si >= ti)
    tri_b16 = tri[0].astype(BF16)
    lane = lax.broadcasted_iota(jnp.int32, (L, LANES), 1)
    lane1 = lax.broadcasted_iota(jnp.int32, (1, LANES), 1)
    lo = lane < SSD_HEADDIM
    lo1 = lane1 < SSD_HEADDIM
    bwd_lanes = lane >= SSD_HEADS
    a_neg = -jnp.exp(alog_ref[...])
    dtb = dtb_ref[...]
    BO, CO = SSD_INNER, SSD_INNER + SSD_GROUPS * SSD_STATE
    NG = SSD_GROUPS
    col_of = lambda d, g, e: d * SSD_HEADS + NG * g + e

    def pair_cols(a, d, g, mask):
        ce = col_of(d, g, 0)
        return jnp.where(mask, a[:, ce:ce + 1], a[:, ce + 1:ce + 2])

    def phase_a(c, carry):
        rows = pl.ds(pl.multiple_of(c * L, L), L)
        dt = _softplus(p_ref[0, rows, DO:DO + LANES] + dtb)
        dta = dt * a_neg
        pre = _cumsum_rows(tri_b16, dta)
        tot = pre[L - 1:L, :]
        cs = jnp.where(bwd_lanes, tot - pre + dta, pre)
        cs_s[rows, :] = cs
        dt_s[rows, :] = dt
        dec = jnp.exp(tot - cs) * dt
        for g in range(NG):
            xp = xc_ref[rows, g * LANES:(g + 1) * LANES]
            Bg = xc_ref[rows, BO + g * LANES:BO + (g + 1) * LANES].astype(BF16)
            xdt = jnp.concatenate([xp * pair_cols(dec, d, g, lo) for d in range(2)], axis=1)
            inc = _dot_tn(Bg, xdt.astype(BF16))
            for d in range(2):
                tot_s[c * 2 * NG + d * NG + g] = pair_cols(tot, d, g, lo1)
                ds_s[c * 2 * NG + d * NG + g] = inc[:, d * LANES:(d + 1) * LANES]
        return carry

    lax.fori_loop(0, nc, phase_a, 0, unroll=4)

    st_ref[...] = jnp.zeros(st_ref.shape, F32)

    def phase_b(i, carry):
        for d in range(2):
            c = i if d == 0 else nc - 1 - i
            for g in range(NG):
                k = d * NG + g
                st = st_ref[k]
                sp_s[c * NG + g, :, d * LANES:(d + 1) * LANES] = st.astype(BF16)
                st_ref[k] = jnp.exp(tot_s[c * 2 * NG + k]) * st + ds_s[c * 2 * NG + k]
        return carry

    lax.fori_loop(0, nc, phase_b, 0)

    dsk = dsk_ref[...]
    ng = ng_ref[...]

    def phase_c(c, carry):
        rows = pl.ds(pl.multiple_of(c * L, L), L)
        cs = cs_s[rows, :]
        cst = cs.T
        dtt = dt_s[rows, :].T
        for g in range(NG):
            pair = slice(g * LANES, (g + 1) * LANES)
            xp = xc_ref[rows, pair]
            xb = xp.astype(BF16)
            Bg = xc_ref[rows, BO + g * LANES:BO + (g + 1) * LANES].astype(BF16)
            Cg = xc_ref[rows, CO + g * LANES:CO + (g + 1) * LANES].astype(BF16)
            CB = _dot_nt(Cg, Bg)
            yi = _dot(Cg, sp_s[c * NG + g])
            bc = [[jnp.broadcast_to(cs[:, col_of(d, g, e):col_of(d, g, e) + 1], (L, LANES))
                   for e in range(2)] for d in range(2)]
            ys = []
            for e in range(2):
                W = None
                for d in range(2):
                    col = col_of(d, g, e)
                    arg = jnp.where(tri[d], bc[d][e] - cst[col:col + 1, :], -jnp.inf)
                    w = jnp.exp(arg) * dtt[col:col + 1, :]
                    W = w if W is None else W + w
                ys.append(_dot((CB * W).astype(BF16), xb))
            y = jnp.where(lo, ys[0], ys[1])
            for d in range(2):
                y = y + yi[:, d * LANES:(d + 1) * LANES] * jnp.exp(jnp.where(lo, bc[d][0], bc[d][1]))
            y = y + xp * dsk[:, pair]
            y = y * _silu(p_ref[0, rows, ZO + g * LANES:ZO + (g + 1) * LANES])
            y = y * lax.rsqrt(jnp.mean(y * y, -1, keepdims=True) + 1e-6) * ng[:, pair]
            y_ref[0, rows, pair] = y.astype(y_ref.dtype)
        return carry

    lax.fori_loop(0, nc, phase_c, 0, unroll=2)


def _ssd(p_ssd, conv_w, conv_b, dt_bias, a_log, d_skip, norm_g):
    B, S, _ = p_ssd.shape
    nc = S // CHUNK
    NG = SSD_GROUPS
    full = lambda a: pl.BlockSpec(a.shape, lambda b: (0, 0))
    return pl.pallas_call(
        _ssd_kernel,
        grid=(B,),
        in_specs=[pl.BlockSpec((1, S, P_SSD_W), lambda b: (b, 0, 0)),
                  full(conv_w), full(conv_b), full(dt_bias), full(a_log), full(d_skip), full(norm_g)],
        out_specs=pl.BlockSpec((1, S, SSD_INNER), lambda b: (b, 0, 0)),
        out_shape=jax.ShapeDtypeStruct((B, S, SSD_INNER), BF16),
        scratch_shapes=[pltpu.VMEM((S + 16, SSD_CONV_DIM), F32), pltpu.VMEM((S, SSD_CONV_DIM), F32),
                        pltpu.VMEM((S, LANES), F32), pltpu.VMEM((S, LANES), F32),
                        pltpu.VMEM((nc * 2 * NG, 1, LANES), F32),
                        pltpu.VMEM((nc * 2 * NG, SSD_STATE, LANES), F32),
                        pltpu.VMEM((nc * NG, SSD_STATE, 2 * LANES), BF16),
                        pltpu.VMEM((2 * NG, SSD_STATE, LANES), F32)],
        compiler_params=_cparams("parallel"),
        name="ssd",
    )(p_ssd, conv_w, conv_b, dt_bias, a_log, d_skip, norm_g)


def _layernorm(u, g, b):
    mu = jnp.mean(u, -1, keepdims=True)
    d = u - mu
    var = jnp.mean(d * d, -1, keepdims=True)
    return d * lax.rsqrt(var + 1e-5) * g + b


def _route(logits):
    lane_i = lax.broadcasted_iota(jnp.int32, logits.shape, 1)
    lane = lane_i.astype(F32)
    group_of_lane = jnp.right_shift(lane_i - N_GROUPS, 2).astype(F32)
    neg = -jnp.inf
    big = 1e6
    glm = jnp.where(lane_i < N_GROUPS, logits, neg)
    gmax = jnp.max(glm, -1, keepdims=True)
    gp = 1.0 / jnp.sum(jnp.exp(glm - gmax), -1, keepdims=True)
    gi = jnp.min(jnp.where(glm == gmax, lane, big), -1, keepdims=True)
    in_group = (lane_i >= N_GROUPS) & (lane_i < N_GROUPS + N_EXPERTS) & (group_of_lane == gi)
    elm = jnp.where(in_group, logits, neg)
    e1 = jnp.max(elm, -1, keepdims=True)
    i1 = jnp.min(jnp.where(elm == e1, lane, big), -1, keepdims=True)
    elm2 = jnp.where(lane == i1, neg, elm)
    e2 = jnp.max(elm2, -1, keepdims=True)
    i2 = jnp.min(jnp.where(elm2 == e2, lane, big), -1, keepdims=True)
    r = jnp.exp(e2 - e1)
    p1 = 1.0 / (1.0 + r)
    p2 = r / (1.0 + r)
    return jnp.where(lane == i1, gp * p1, jnp.where(lane == i2, gp * p2, 0.0))


OUT_SUB = 256


def _outproj_kernel(alpha, x_ref, ya_ref, yb_ref, yc_ref, wo_ref, g_ref, b_ref, wr_ref, br_ref,
                    x1_ref, cmb_ref):
    tm = x_ref.shape[0]
    for r in range(tm // OUT_SUB):
        rows = slice(r * OUT_SUB, (r + 1) * OUT_SUB)
        mix = (_dot(ya_ref[rows, :], wo_ref[0:MLA_OUT, :])
               + _dot(yb_ref[rows, :], wo_ref[MLA_OUT:MLA_OUT + MLSTM_W, :])
               + _dot(yc_ref[rows, :], wo_ref[MLA_OUT + MLSTM_W:, :]))
        x1 = _layernorm(alpha * x_ref[rows, :] + mix, g_ref[...], b_ref[...])
        x1_ref[rows, :] = x1
        x_hi = x1.astype(BF16)
        x_lo = (x1 - x_hi.astype(F32)).astype(BF16)
        a = _dot(x_hi, wr_ref[...])
        c = _dot(x_lo, wr_ref[...])
        logits = (a[:, 0:LANES] + (a[:, LANES:] + c[:, 0:LANES]) + c[:, LANES:]) + br_ref[...]
        cmb_ref[rows, :] = _route(logits)


def _outproj(alpha, x2, ya, yb, yc, wo, g, b, wr, br, tm=1024):
    T, D = x2.shape
    row = lambda w: pl.BlockSpec((tm, w), lambda i: (i, 0))
    full = lambda a: pl.BlockSpec(a.shape, lambda i: (0, 0))
    return pl.pallas_call(
        functools.partial(_outproj_kernel, alpha),
        grid=(T // tm,),
        in_specs=[row(D), row(MLA_OUT), row(MLSTM_W), row(SSD_INNER), full(wo), full(g), full(b),
                  full(wr), full(br)],
        out_specs=[row(D), row(LANES)],
        out_shape=[jax.ShapeDtypeStruct((T, D), F32), jax.ShapeDtypeStruct((T, LANES), F32)],
        compiler_params=_cparams("parallel"),
        name="outproj_ln_router",
    )(x2, ya, yb, yc, wo, g, b, wr, br)


MOE_SUB = 512


def _moe_kernel(alpha, x_ref, cmb_ref, wg_ref, wu_ref, wd_ref, g_ref, b_ref, o_ref, xb_ref, acc_ref):
    grp = pl.program_id(1)

    @pl.when(grp == 0)
    def _():
        xb_ref[...] = x_ref[...].astype(BF16)
        acc_ref[...] = jnp.zeros(acc_ref.shape, F32)

    lane = lax.broadcasted_iota(jnp.int32, (MOE_SUB, LANES), 1)
    f = wg_ref.shape[-1] // EXPERTS_PER_GROUP
    for r in range(x_ref.shape[0] // MOE_SUB):
        rows = slice(r * MOE_SUB, (r + 1) * MOE_SUB)
        xb = xb_ref[rows, :]
        cmb = cmb_ref[rows, :]
        cexp = []
        for e in range(EXPERTS_PER_GROUP):
            col = N_GROUPS + grp * EXPERTS_PER_GROUP + e
            c = jnp.sum(jnp.where(lane == col, cmb, 0.0), -1, keepdims=True)
            cexp.append(jnp.broadcast_to(c, (MOE_SUB, f)))
        h = _silu(_dot(xb, wg_ref[0])) * _dot(xb, wu_ref[0]) * jnp.concatenate(cexp, axis=1)
        acc_ref[rows, :] += _dot(h.astype(BF16), wd_ref[0])

    @pl.when(grp == N_GROUPS - 1)
    def _():
        o_ref[...] = _layernorm(alpha * x_ref[...] + acc_ref[...], g_ref[...], b_ref[...])


def _moe(alpha, x1, cmb, wg, wu, wd, g, b, tm=1024):
    T, D = x1.shape
    GF = wg.shape[-1]
    return pl.pallas_call(
        functools.partial(_moe_kernel, alpha),
        grid=(T // tm, N_GROUPS),
        in_specs=[pl.BlockSpec((tm, D), lambda i, e: (i, 0)),
                  pl.BlockSpec((tm, LANES), lambda i, e: (i, 0)),
                  pl.BlockSpec((1, D, GF), lambda i, e: (e, 0, 0)),
                  pl.BlockSpec((1, D, GF), lambda i, e: (e, 0, 0)),
                  pl.BlockSpec((1, GF, D), lambda i, e: (e, 0, 0)),
                  pl.BlockSpec((1, D), lambda i, e: (0, 0)),
                  pl.BlockSpec((1, D), lambda i, e: (0, 0))],
        out_specs=pl.BlockSpec((tm, D), lambda i, e: (i, 0)),
        out_shape=jax.ShapeDtypeStruct((T, D), F32),
        scratch_shapes=[pltpu.VMEM((tm, D), BF16), pltpu.VMEM((tm, D), F32)],
        compiler_params=_cparams("parallel", "arbitrary"),
        name="moe_ln",
    )(x1, cmb, wg, wu, wd, g, b)


def _pad_lanes(a, width=LANES):
    return jnp.pad(a, [(0, 0)] * (a.ndim - 1) + [(0, width - a.shape[-1])])


def kernel(x, positions, w_in, mla_q_norm, mla_kv_norm, mla_w_uq, mla_w_ukv, mlstm_gate_bias, mlstm_norm,
           ssd_conv_w, ssd_conv_b, ssd_dt_bias, ssd_a_log, ssd_d, ssd_norm, w_out, ln1_g, ln1_b,
           router_group_w, router_group_b, router_expert_w, router_expert_b,
           expert_w_gate, expert_w_up, expert_w_down, ln2_g, ln2_b):
    B, S, D = x.shape
    depth = w_in.shape[0]
    T = B * S
    alpha = (2 * depth) ** 0.25

    inv = ROPE_BASE ** (-jnp.arange(0, MLA_ROPE, 2, dtype=F32) / MLA_ROPE)
    ang = positions.astype(F32).reshape(T, 1) * inv
    cos, sin = jnp.cos(ang), jnp.sin(ang)
    ones = jnp.ones((T, MLA_NOPE), F32)
    zeros = jnp.zeros((T, MLA_NOPE), F32)
    pad = jnp.zeros((T, LANES - MLA_NOPE - MLA_ROPE), F32)
    cos_t = jnp.concatenate([ones, cos, cos, pad], -1)
    sin_t = jnp.concatenate([zeros, -sin, sin, pad], -1)

    o = [0]
    for s in (MLA_Q_RANK, MLA_KV_RANK, MLA_ROPE, MLSTM_W, MLSTM_W, MLSTM_W, MLSTM_W, 4 * MLSTM_HEADS,
              SSD_INNER, SSD_CONV_DIM, 2 * SSD_HEADS):
        o.append(o[-1] + s)
    zc = lambda n: jnp.zeros((depth, D, n), w_in.dtype)
    w_all = jnp.concatenate(
        [w_in[..., o[0]:o[2]], zc(MLA_NOPE), w_in[..., o[2]:o[3]], zc(LANES - MLA_NOPE - MLA_ROPE),
         w_in[..., o[3]:o[4]], w_in[..., o[4]:o[5]] * (MLSTM_DH ** -0.5), w_in[..., o[5]:o[6]],
         w_in[..., o[6]:o[8]], zc(LANES - 4 * MLSTM_HEADS),
         w_in[..., o[8]:o[11]], zc(LANES - 2 * SSD_HEADS)], -1).astype(BF16)
    head_of_lane = jnp.arange(LANES) // MLSTM_DH
    avg_blk = (head_of_lane[:, None] == head_of_lane[None, :]).astype(BF16) * (1.0 / MLSTM_DH)
    ml_avg = jnp.concatenate([avg_blk, avg_blk], 0)

    wq = mla_w_uq.reshape(depth, MLA_Q_RANK, MLA_HEADS, MLA_NOPE + MLA_ROPE)
    wq = _pad_lanes(wq).reshape(depth, MLA_Q_RANK, MLA_HEADS * LANES).astype(BF16)
    wkv = mla_w_ukv.reshape(depth, MLA_KV_RANK, MLA_HEADS, MLA_NOPE + MLA_V)
    wk = _pad_lanes(wkv[..., :MLA_NOPE]).reshape(depth, MLA_KV_RANK, MLA_HEADS * LANES).astype(BF16)
    wv = wkv[..., MLA_NOPE:].reshape(depth, MLA_KV_RANK, MLA_HEADS // 2, 2 * MLA_V)
    wv = _pad_lanes(wv, 2 * LANES).reshape(depth, MLA_KV_RANK, MLA_HEADS * LANES).astype(BF16)

    gate_bias = _pad_lanes(mlstm_gate_bias.reshape(depth, 1, 4 * MLSTM_HEADS))
    dt_bias = _pad_lanes(ssd_dt_bias.reshape(depth, 1, 2 * SSD_HEADS))
    a_log = _pad_lanes(ssd_a_log.reshape(depth, 1, 2 * SSD_HEADS))
    d_skip = jnp.repeat(ssd_d, SSD_HEADDIM, axis=-1).reshape(depth, 1, SSD_INNER)
    conv_w = jnp.pad(ssd_conv_w, ((0, 0), (0, 8 - SSD_CONV), (0, 0)))

    w_o = w_out.astype(BF16)
    w_r = _pad_lanes(jnp.concatenate(
        [router_group_w, router_expert_w.transpose(0, 2, 1, 3).reshape(depth, D, N_EXPERTS)], -1))
    w_r_hi = w_r.astype(BF16)
    w_r2 = jnp.concatenate([w_r_hi, (w_r - w_r_hi.astype(F32)).astype(BF16)], -1)
    b_r = _pad_lanes(jnp.concatenate(
        [router_group_b, router_expert_b.reshape(depth, N_EXPERTS)], -1).reshape(depth, 1, -1))
    F = expert_w_gate.shape[-1]
    wide = lambda w: w.reshape(depth, N_GROUPS, EXPERTS_PER_GROUP, D, F).transpose(0, 1, 3, 2, 4).reshape(
        depth, N_GROUPS, D, EXPERTS_PER_GROUP * F).astype(BF16)
    e_g = wide(expert_w_gate)
    e_u = wide(expert_w_up)
    e_d = expert_w_down.reshape(depth, N_GROUPS, EXPERTS_PER_GROUP * F, D).astype(BF16)
    row = lambda a, l: a[l].reshape(1, -1)

    x2 = x.reshape(T, D)
    for l in range(depth):
        q, k, v, p_qkv, p_og, p_ssd = _inproj(x2, w_all[l], cos_t, sin_t,
                                              row(mla_q_norm, l), row(mla_kv_norm, l), wq[l], wk[l], wv[l])
        y_a = _attention(q.reshape(B, S, -1), k.reshape(B, S, -1), v.reshape(B, S, -1))
        y_b = _mlstm(p_qkv.reshape(B, S, -1), p_og.reshape(B, S, -1), gate_bias[l], row(mlstm_norm, l), ml_avg)
        y_c = _ssd(p_ssd.reshape(B, S, -1), conv_w[l], row(ssd_conv_b, l), dt_bias[l], a_log[l],
                   d_skip[l], row(ssd_norm, l))
        x1, cmb = _outproj(alpha, x2, y_a.reshape(T, -1), y_b.reshape(T, -1), y_c.reshape(T, -1),
                           w_o[l], row(ln1_g, l), row(ln1_b, l), w_r2[l], b_r[l])
        x2 = _moe(alpha, x1, cmb, e_g[l], e_u[l], e_d[l], row(ln2_g, l), row(ln2_b, l))
    return x2.reshape(B, S, D)
```

```python
import functools

import jax
import jax.numpy as jnp
from jax import lax
from jax.experimental import pallas as pl
from jax.experimental.pallas import tpu as pltpu

F32 = jnp.float32
BF16 = jnp.bfloat16

LANES = 128
VMEM_LIMIT = 56 * 1024 * 1024

MLA_HEADS = 8
MLA_NOPE = 64
MLA_ROPE = 32
MLA_V = 64
MLA_Q_RANK = 256
MLA_KV_RANK = 128
ROPE_BASE = 10000.0
MLSTM_HEADS = 4
MLSTM_DH = 64
SSD_HEADS = 4
SSD_HEADDIM = 64
SSD_GROUPS = 2
SSD_STATE = 128
SSD_CONV = 5
N_GROUPS = 4
EXPERTS_PER_GROUP = 4
N_EXPERTS = 16
M_INIT = -1e30
LOG2E = 1.4426950408889634

MLA_OUT = MLA_HEADS * MLA_V
MLSTM_W = MLSTM_HEADS * MLSTM_DH
SSD_INNER = SSD_HEADS * SSD_HEADDIM
SSD_CONV_DIM = SSD_INNER + 2 * SSD_GROUPS * SSD_STATE
CHUNK = 128

P_MLA_W = 512
P_SSD_W = SSD_INNER + SSD_CONV_DIM + LANES


def _cparams(*sem):
    return pltpu.CompilerParams(dimension_semantics=sem, vmem_limit_bytes=VMEM_LIMIT)


def _dot(a, b):
    return jnp.dot(a, b, preferred_element_type=F32)


def _dot_nt(a, b):
    return lax.dot_general(a, b, (((1,), (1,)), ((), ())), preferred_element_type=F32)


def _dot_tn(a, b):
    return lax.dot_general(a, b, (((0,), (0,)), ((), ())), preferred_element_type=F32)


def _split3(x):
    hi = x.astype(BF16)
    r = x - hi.astype(F32)
    mid = r.astype(BF16)
    lo = (r - mid.astype(F32)).astype(BF16)
    return hi, mid, lo


def _cumsum_rows(tri_b16, x):
    hi, mid, lo = _split3(x)
    c = _dot(tri_b16, jnp.concatenate([hi, mid, lo], axis=1))
    return c[:, 2 * LANES:3 * LANES] + c[:, LANES:2 * LANES] + c[:, 0:LANES]


def _softplus(x):
    return jnp.maximum(x, 0.0) + jnp.log1p(jnp.exp(-jnp.abs(x)))


def _sigmoid(x):
    return 1.0 / (1.0 + jnp.exp(-x))


def _silu(x):
    return x * _sigmoid(x)


IN_COLS = (0, P_MLA_W, P_MLA_W + 3 * MLSTM_W, P_MLA_W + 4 * MLSTM_W + LANES,
           P_MLA_W + 4 * MLSTM_W + LANES + P_SSD_W)


def _inproj_kernel(x_ref, w_ref, pa_ref, pq_ref, pg_ref, pc_ref):
    x = x_ref[...].astype(BF16)
    c = IN_COLS
    pa_ref[...] = _dot(x, w_ref[:, c[0]:c[1]])
    pq_ref[...] = _dot(x, w_ref[:, c[1]:c[2]]).astype(pq_ref.dtype)
    pg_ref[...] = _dot(x, w_ref[:, c[2]:c[3]])
    pc_ref[...] = _dot(x, w_ref[:, c[3]:c[4]])


def _inproj(x2, w_all, tm=512):
    T, D = x2.shape
    row = lambda w: pl.BlockSpec((tm, w), lambda i: (i, 0))
    return pl.pallas_call(
        _inproj_kernel,
        grid=(T // tm,),
        in_specs=[row(D), pl.BlockSpec(w_all.shape, lambda i: (0, 0))],
        out_specs=[row(P_MLA_W), row(3 * MLSTM_W), row(MLSTM_W + LANES), row(P_SSD_W)],
        out_shape=[jax.ShapeDtypeStruct((T, P_MLA_W), F32),
                   jax.ShapeDtypeStruct((T, 3 * MLSTM_W), BF16),
                   jax.ShapeDtypeStruct((T, MLSTM_W + LANES), F32),
                   jax.ShapeDtypeStruct((T, P_SSD_W), F32)],
        compiler_params=_cparams("parallel"),
        name="inproj",
    )(x2, w_all)


def _mla_prep_kernel(p_ref, cos_ref, sin_ref, qn_ref, kvn_ref, wq_ref, wk_ref, wv_ref,
                     q_ref, k_ref, v_ref):
    p = p_ref[...]
    cq = p[:, 0:MLA_Q_RANK]
    ckv = p[:, MLA_Q_RANK:MLA_Q_RANK + MLA_KV_RANK]
    krb = p[:, MLA_Q_RANK + MLA_KV_RANK:P_MLA_W]
    cqn = (cq * lax.rsqrt(jnp.mean(cq * cq, -1, keepdims=True) + 1e-6) * qn_ref[...]).astype(BF16)
    ckn = (ckv * lax.rsqrt(jnp.mean(ckv * ckv, -1, keepdims=True) + 1e-6) * kvn_ref[...]).astype(BF16)
    q = _dot(cqn, wq_ref[...])
    k = _dot(ckn, wk_ref[...])
    v = _dot(ckn, wv_ref[...])
    vlane = lax.broadcasted_iota(jnp.int32, v.shape, 1)
    v_ref[...] = jnp.where(vlane % (2 * LANES) >= LANES, 1.0, v).astype(v_ref.dtype)
    cos = cos_ref[...]
    sin = sin_ref[...]
    lane = lax.broadcasted_iota(jnp.int32, cos.shape, 1)
    first_half = lane < MLA_NOPE + MLA_ROPE // 2

    def rope(t):
        partner = jnp.where(first_half, pltpu.roll(t, LANES - MLA_ROPE // 2, 1),
                            pltpu.roll(t, MLA_ROPE // 2, 1))
        return t * cos + partner * sin

    scale = (MLA_NOPE + MLA_ROPE) ** -0.5 * LOG2E
    kr = rope(krb)
    for h in range(MLA_HEADS):
        sl = slice(h * LANES, (h + 1) * LANES)
        q_ref[:, sl] = (rope(q[:, sl]) * scale).astype(q_ref.dtype)
        k_ref[:, sl] = (k[:, sl] + kr).astype(k_ref.dtype)


def _mla_prep(p_mla, cos, sin, qn, kvn, wq, wk, wv, tm=512):
    T = p_mla.shape[0]
    row = lambda w: pl.BlockSpec((tm, w), lambda i: (i, 0))
    full = lambda a: pl.BlockSpec(a.shape, lambda i: (0, 0))
    HW = MLA_HEADS * LANES
    return pl.pallas_call(
        _mla_prep_kernel,
        grid=(T // tm,),
        in_specs=[row(P_MLA_W), row(LANES), row(LANES), full(qn), full(kvn), full(wq), full(wk), full(wv)],
        out_specs=[row(HW), row(HW), row(HW)],
        out_shape=[jax.ShapeDtypeStruct((T, HW), BF16)] * 3,
        compiler_params=_cparams("parallel"),
        name="mla_prep",
    )(p_mla, cos, sin, qn, kvn, wq, wk, wv)


ATTN_SUB = 128


def _attn_kernel(q_ref, k_ref, v_ref, o_ref):
    tq = q_ref.shape[1]
    lane = lax.broadcasted_iota(jnp.int32, (ATTN_SUB, LANES), 1)
    for r in range(tq // ATTN_SUB):
        rows = slice(r * ATTN_SUB, (r + 1) * ATTN_SUB)
        outs = []
        for hh in range(2):
            sl = slice(hh * LANES, (hh + 1) * LANES)
            s = _dot_nt(q_ref[0, rows, sl], k_ref[0, :, sl])
            p = jnp.exp2(s - jnp.max(s, -1, keepdims=True)).astype(BF16)
            ov = _dot(p, v_ref[0])
            outs.append(ov[:, 0:LANES] / ov[:, LANES:])
        o_ref[0, rows, :] = jnp.where(lane < MLA_V, outs[0], outs[1]).astype(o_ref.dtype)


def _attention(q, k, v):
    B, S, _ = q.shape
    tq = S
    return pl.pallas_call(
        _attn_kernel,
        grid=(B, MLA_HEADS // 2, S // tq),
        in_specs=[pl.BlockSpec((1, tq, 2 * LANES), lambda b, j, i: (b, i, j)),
                  pl.BlockSpec((1, S, 2 * LANES), lambda b, j, i: (b, 0, j)),
                  pl.BlockSpec((1, S, 2 * LANES), lambda b, j, i: (b, 0, j))],
        out_specs=pl.BlockSpec((1, tq, LANES), lambda b, j, i: (b, i, j)),
        out_shape=jax.ShapeDtypeStruct((B, S, MLA_OUT), BF16),
        compiler_params=_cparams("parallel", "parallel", "arbitrary"),
        name="mla_attn",
    )(q, k, v)


_ML_FCOL = lambda d, h: 8 * d + MLSTM_HEADS + h
_ML_ICOL = lambda d, h: 8 * d + h


def _scan_max_rows(x, row, bwd_lanes):
    L = x.shape[0]
    neg = -jnp.inf
    xf, xb = x, x
    k = 1
    while k < L:
        xf = jnp.maximum(xf, jnp.where(row >= k, pltpu.roll(xf, k, 0), neg))
        xb = jnp.maximum(xb, jnp.where(row < L - k, pltpu.roll(xb, L - k, 0), neg))
        k *= 2
    return jnp.where(bwd_lanes, xb, xf)


def _mlstm_kernel(qkv_ref, og_ref, gb_ref, ng_ref, avg_ref, y_ref,
                  u_s, b_s, tot_s, mdec_s, mpf_s, mpb_s, ds_s, sp_s, st_s):
    S = qkv_ref.shape[1]
    L = CHUNK
    nc = S // L
    H = MLSTM_HEADS
    SW = 2 * LANES
    ti = lax.broadcasted_iota(jnp.int32, (L, L), 0)
    si = lax.broadcasted_iota(jnp.int32, (L, L), 1)
    tri = (si <= ti, si >= ti)
    tri_b16 = tri[0].astype(BF16)
    lane = lax.broadcasted_iota(jnp.int32, (L, LANES), 1)
    row = lax.broadcasted_iota(jnp.int32, (L, LANES), 0)
    lane1 = lax.broadcasted_iota(jnp.int32, (1, LANES), 1)
    bwd_lanes = lane >= 2 * H
    bwd_lanes1 = lane1 >= 2 * H
    lo = lane < MLSTM_DH
    gbias = gb_ref[...]
    ones_b16 = jnp.ones((L, LANES), BF16)
    zero_b16 = jnp.zeros((L, LANES), BF16)
    QO, KO, VO = 0, MLSTM_W, 2 * MLSTM_W
    OO, GO = 0, MLSTM_W

    def head_slices(rows, j, e, off):
        sel = lo if e == 0 else ~lo
        return jnp.where(sel, qkv_ref[0, rows, off + j * LANES:off + (j + 1) * LANES], zero_b16)

    def phase_a(c, carry):
        rows = pl.ds(pl.multiple_of(c * L, L), L)
        G = og_ref[0, rows, GO:GO + LANES] + gbias
        LF = -_softplus(-G)
        pre = _cumsum_rows(tri_b16, LF)
        tot = pre[L - 1:L, :]
        Bm = jnp.where(bwd_lanes, tot - pre + LF, pre)
        U = pltpu.roll(G, H, 1) - Bm
        u_s[rows, :] = U
        b_s[rows, :] = Bm
        dec = tot + U
        mdec = jnp.max(dec, 0, keepdims=True)
        W = jnp.exp(dec - mdec)
        tot_s[c] = tot
        mdec_s[c] = mdec
        for j in range(H // 2):
            for e in range(2):
                h = 2 * j + e
                km = head_slices(rows, j, e, KO)
                vm = head_slices(rows, j, e, VO).astype(F32)
                parts = []
                for d in range(2):
                    wb = jnp.broadcast_to(W[:, _ML_FCOL(d, h):_ML_FCOL(d, h) + 1], (L, LANES))
                    parts += [wb * vm, wb]
                wv = jnp.concatenate(parts, axis=1).astype(BF16)
                ds_s[c * H + h] = _dot_tn(km, wv)
        return carry

    lax.fori_loop(0, nc, phase_a, 0, unroll=2)

    st_s[...] = jnp.zeros(st_s.shape, F32)

    def phase_b(i, m):
        cf, cb = i, nc - 1 - i
        totm = jnp.where(bwd_lanes1, tot_s[cb], tot_s[cf])
        mdecm = jnp.where(bwd_lanes1, mdec_s[cb], mdec_s[cf])
        mpf_s[cf] = m
        mpb_s[cb] = m
        m_new = jnp.maximum(totm + m, mdecm)
        w_c = jnp.exp(totm + m - m_new)
        w_d = jnp.exp(mdecm - m_new)
        for h in range(H):
            for d, c in ((0, cf), (1, cb)):
                part = slice(d * SW, (d + 1) * SW)
                col = slice(_ML_FCOL(d, h), _ML_FCOL(d, h) + 1)
                st = st_s[h, :, part]
                sp_s[c * H + h, :, part] = st.astype(BF16)
                st_s[h, :, part] = w_c[:, col] * st + w_d[:, col] * ds_s[c * H + h, :, part]
        return m_new

    lax.fori_loop(0, nc, phase_b, jnp.full((1, LANES), M_INIT, F32))

    ng = ng_ref[...]
    avg = avg_ref[...]

    def head_mean(t):
        hi = t.astype(BF16)
        lo_part = (t - hi.astype(F32)).astype(BF16)
        return _dot(jnp.concatenate([hi, lo_part], axis=1), avg)

    def phase_c(c, carry):
        rows = pl.ds(pl.multiple_of(c * L, L), L)
        U = u_s[rows, :]
        Bm = b_s[rows, :]
        UT = U.T
        m_prev = jnp.where(bwd_lanes1, mpb_s[c], mpf_s[c])
        Gm = Bm + m_prev
        Mt = jnp.maximum(Gm, Bm + _scan_max_rows(U, row, bwd_lanes))
        Z = Bm - Mt
        WI = jnp.exp(Gm - Mt)
        FL = jnp.exp(-Mt)
        for j in range(H // 2):
            pair = slice(j * LANES, (j + 1) * LANES)
            kb = qkv_ref[0, rows, KO + j * LANES:KO + (j + 1) * LANES]
            hsum = []
            for e in range(2):
                h = 2 * j + e
                qm = head_slices(rows, j, e, QO)
                vaug = jnp.concatenate([head_slices(rows, j, e, VO), ones_b16], axis=1)
                qk = _dot_nt(qm, kb)
                inter = _dot(qm, sp_s[c * H + h])
                ps = []
                for d in range(2):
                    fc = _ML_FCOL(d, h)
                    e_ts = jnp.where(tri[d], UT[fc:fc + 1, :], -jnp.inf) + Z[:, fc:fc + 1]
                    ps.append((qk * jnp.exp(e_ts)).astype(BF16))
                pv = _dot(jnp.concatenate(ps, axis=0), vaug)
                hs = None
                for d in range(2):
                    fc = _ML_FCOL(d, h)
                    nd = pv[d * L:(d + 1) * L] + WI[:, fc:fc + 1] * inter[:, d * SW:(d + 1) * SW]
                    hd = nd[:, 0:LANES] / jnp.maximum(jnp.abs(nd[:, LANES:SW]), FL[:, fc:fc + 1])
                    hs = hd if hs is None else hs + hd
                hsum.append(hs)
            hp = jnp.where(lo, hsum[0], hsum[1])
            dlt = hp - head_mean(hp)
            var = head_mean(dlt * dlt)
            hn = dlt * lax.rsqrt(var + 1e-5) * ng[:, pair]
            o = og_ref[0, rows, OO + j * LANES:OO + (j + 1) * LANES]
            y_ref[0, rows, pair] = (_sigmoid(o) * hn).astype(y_ref.dtype)
        return carry

    lax.fori_loop(0, nc, phase_c, 0, unroll=2)


def _mlstm(p_qkv, p_og, gate_bias, norm_g, avg):
    B, S, _ = p_qkv.shape
    nc = S // CHUNK
    H = MLSTM_HEADS
    vec = lambda: pltpu.VMEM((nc, 1, LANES), F32)
    return pl.pallas_call(
        _mlstm_kernel,
        grid=(B,),
        in_specs=[pl.BlockSpec((1, S, 3 * MLSTM_W), lambda b: (b, 0, 0)),
                  pl.BlockSpec((1, S, MLSTM_W + LANES), lambda b: (b, 0, 0)),
                  pl.BlockSpec((1, LANES), lambda b: (0, 0)),
                  pl.BlockSpec((1, MLSTM_W), lambda b: (0, 0)),
                  pl.BlockSpec(avg.shape, lambda b: (0, 0))],
        out_specs=pl.BlockSpec((1, S, MLSTM_W), lambda b: (b, 0, 0)),
        out_shape=jax.ShapeDtypeStruct((B, S, MLSTM_W), BF16),
        scratch_shapes=[pltpu.VMEM((S, LANES), F32), pltpu.VMEM((S, LANES), F32),
                        vec(), vec(), vec(), vec(),
                        pltpu.VMEM((nc * H, LANES, 4 * LANES), F32),
                        pltpu.VMEM((nc * H, LANES, 4 * LANES), BF16),
                        pltpu.VMEM((H, LANES, 4 * LANES), F32)],
        compiler_params=_cparams("parallel"),
        name="mlstm",
    )(p_qkv, p_og, gate_bias, norm_g, avg)


def _ssd_kernel(p_ref, cw_ref, cb_ref, dtb_ref, alog_ref, dsk_ref, ng_ref, y_ref,
                xpad_ref, xc_ref, cs_s, dt_s, tot_s, ds_s, sp_s, st_ref):
    S = p_ref.shape[1]
    L = CHUNK
    nc = S // L
    PAD = 8
    ZO, XO, DO = 0, SSD_INNER, SSD_INNER + SSD_CONV_DIM

    zpad = jnp.zeros((PAD, SSD_CONV_DIM), F32)
    xpad_ref[0:PAD, :] = zpad
    xpad_ref[PAD + S:PAD + S + PAD, :] = zpad
    R = 256
    for r in range(S // R):
        xpad_ref[PAD + r * R:PAD + (r + 1) * R, :] = p_ref[0, r * R:(r + 1) * R, XO:XO + SSD_CONV_DIM]
    half = SSD_CONV // 2
    for r in range(S // R):
        win = xpad_ref[r * R:r * R + R + 2 * PAD, :]
        acc = jnp.zeros((R, SSD_CONV_DIM), F32) + cb_ref[...]
        for kk in range(SSD_CONV):
            sh = (half - kk) % (R + 2 * PAD)
            tap = win if sh == 0 else pltpu.roll(win, sh, 0)
            acc = acc + tap[PAD:PAD + R, :] * cw_ref[kk:kk + 1, :]
        xc_ref[r * R:(r + 1) * R, :] = _silu(acc)

    ti = lax.broadcasted_iota(jnp.int32, (L, L), 0)
    si = lax.broadcasted_iota(jnp.int32, (L, L), 1)
    tri = (si <= ti, si >= ti)
    tri_b16 = tri[0].astype(BF16)
    lane = lax.broadcasted_iota(jnp.int32, (L, LANES), 1)
    lane1 = lax.broadcasted_iota(jnp.int32, (1, LANES), 1)
    lo = lane < SSD_HEADDIM
    lo1 = lane1 < SSD_HEADDIM
    bwd_lanes = lane >= SSD_HEADS
    a_neg = -jnp.exp(alog_ref[...])
    dtb = dtb_ref[...]
    BO, CO = SSD_INNER, SSD_INNER + SSD_GROUPS * SSD_STATE
    NG = SSD_GROUPS
    col_of = lambda d, g, e: d * SSD_HEADS + NG * g + e

    def pair_cols(a, d, g, mask):
        ce = col_of(d, g, 0)
        return jnp.where(mask, a[:, ce:ce + 1], a[:, ce + 1:ce + 2])

    def phase_a(c, carry):
        rows = pl.ds(pl.multiple_of(c * L, L), L)
        dt = _softplus(p_ref[0, rows, DO:DO + LANES] + dtb)
        dta = dt * a_neg
        pre = _cumsum_rows(tri_b16, dta)
        tot = pre[L - 1:L, :]
        cs = jnp.where(bwd_lanes, tot - pre + dta, pre)
        cs_s[rows, :] = cs
        dt_s[rows, :] = dt
        dec = jnp.exp(tot - cs) * dt
        for g in range(NG):
            xp = xc_ref[rows, g * LANES:(g + 1) * LANES]
            Bg = xc_ref[rows, BO + g * LANES:BO + (g + 1) * LANES].astype(BF16)
            xdt = jnp.concatenate([xp * pair_cols(dec, d, g, lo) for d in range(2)], axis=1)
            inc = _dot_tn(Bg, xdt.astype(BF16))
            for d in range(2):
                tot_s[c * 2 * NG + d * NG + g] = pair_cols(tot, d, g, lo1)
                ds_s[c * 2 * NG + d * NG + g] = inc[:, d * LANES:(d + 1) * LANES]
        return carry

    lax.fori_loop(0, nc, phase_a, 0, unroll=4)

    st_ref[...] = jnp.zeros(st_ref.shape, F32)

    def phase_b(i, carry):
        for d in range(2):
            c = i if d == 0 else nc - 1 - i
            for g in range(NG):
                k = d * NG + g
                st = st_ref[k]
                sp_s[c * NG + g, :, d * LANES:(d + 1) * LANES] = st.astype(BF16)
                st_ref[k] = jnp.exp(tot_s[c * 2 * NG + k]) * st + ds_s[c * 2 * NG + k]
        return carry

    lax.fori_loop(0, nc, phase_b, 0)

    dsk = dsk_ref[...]
    ng = ng_ref[...]

    def phase_c(c, carry):
        rows = pl.ds(pl.multiple_of(c * L, L), L)
        cs = cs_s[rows, :]
        cst = cs.T
        dtt = dt_s[rows, :].T
        for g in range(NG):
            pair = slice(g * LANES, (g + 1) * LANES)
            xp = xc_ref[rows, pair]
            xb = xp.astype(BF16)
            Bg = xc_ref[rows, BO + g * LANES:BO + (g + 1) * LANES].astype(BF16)
            Cg = xc_ref[rows, CO + g * LANES:CO + (g + 1) * LANES].astype(BF16)
            CB = _dot_nt(Cg, Bg)
            yi = _dot(Cg, sp_s[c * NG + g])
            bc = [[jnp.broadcast_to(cs[:, col_of(d, g, e):col_of(d, g, e) + 1], (L, LANES))
                   for e in range(2)] for d in range(2)]
            ys = []
            for e in range(2):
                W = None
                for d in range(2):
                    col = col_of(d, g, e)
                    arg = jnp.where(tri[d], bc[d][e] - cst[col:col + 1, :], -jnp.inf)
                    w = jnp.exp(arg) * dtt[col:col + 1, :]
                    W = w if W is None else W + w
                ys.append(_dot((CB * W).astype(BF16), xb))
            y = jnp.where(lo, ys[0], ys[1])
            for d in range(2):
                y = y + yi[:, d * LANES:(d + 1) * LANES] * jnp.exp(jnp.where(lo, bc[d][0], bc[d][1]))
            y = y + xp * dsk[:, pair]
            y = y * _silu(p_ref[0, rows, ZO + g * LANES:ZO + (g + 1) * LANES])
            y = y * lax.rsqrt(jnp.mean(y * y, -1, keepdims=True) + 1e-6) * ng[:, pair]
            y_ref[0, rows, pair] = y.astype(y_ref.dtype)
        return carry

    lax.fori_loop(0, nc, phase_c, 0, unroll=2)


def _ssd(p_ssd, conv_w, conv_b, dt_bias, a_log, d_skip, norm_g):
    B, S, _ = p_ssd.shape
    nc = S // CHUNK
    NG = SSD_GROUPS
    full = lambda a: pl.BlockSpec(a.shape, lambda b: (0, 0))
    return pl.pallas_call(
        _ssd_kernel,
        grid=(B,),
        in_specs=[pl.BlockSpec((1, S, P_SSD_W), lambda b: (b, 0, 0)),
                  full(conv_w), full(conv_b), full(dt_bias), full(a_log), full(d_skip), full(norm_g)],
        out_specs=pl.BlockSpec((1, S, SSD_INNER), lambda b: (b, 0, 0)),
        out_shape=jax.ShapeDtypeStruct((B, S, SSD_INNER), BF16),
        scratch_shapes=[pltpu.VMEM((S + 16, SSD_CONV_DIM), F32), pltpu.VMEM((S, SSD_CONV_DIM), F32),
                        pltpu.VMEM((S, LANES), F32), pltpu.VMEM((S, LANES), F32),
                        pltpu.VMEM((nc * 2 * NG, 1, LANES), F32),
                        pltpu.VMEM((nc * 2 * NG, SSD_STATE, LANES), F32),
                        pltpu.VMEM((nc * NG, SSD_STATE, 2 * LANES), BF16),
                        pltpu.VMEM((2 * NG, SSD_STATE, LANES), F32)],
        compiler_params=_cparams("parallel"),
        name="ssd",
    )(p_ssd, conv_w, conv_b, dt_bias, a_log, d_skip, norm_g)


def _layernorm(u, g, b):
    mu = jnp.mean(u, -1, keepdims=True)
    d = u - mu
    var = jnp.mean(d * d, -1, keepdims=True)
    return d * lax.rsqrt(var + 1e-5) * g + b


def _route(logits):
    lane_i = lax.broadcasted_iota(jnp.int32, logits.shape, 1)
    lane = lane_i.astype(F32)
    group_of_lane = jnp.right_shift(lane_i - N_GROUPS, 2).astype(F32)
    neg = -jnp.inf
    big = 1e6
    glm = jnp.where(lane_i < N_GROUPS, logits, neg)
    gmax = jnp.max(glm, -1, keepdims=True)
    gp = 1.0 / jnp.sum(jnp.exp(glm - gmax), -1, keepdims=True)
    gi = jnp.min(jnp.where(glm == gmax, lane, big), -1, keepdims=True)
    in_group = (lane_i >= N_GROUPS) & (lane_i < N_GROUPS + N_EXPERTS) & (group_of_lane == gi)
    elm = jnp.where(in_group, logits, neg)
    e1 = jnp.max(elm, -1, keepdims=True)
    i1 = jnp.min(jnp.where(elm == e1, lane, big), -1, keepdims=True)
    elm2 = jnp.where(lane == i1, neg, elm)
    e2 = jnp.max(elm2, -1, keepdims=True)
    i2 = jnp.min(jnp.where(elm2 == e2, lane, big), -1, keepdims=True)
    r = jnp.exp(e2 - e1)
    p1 = 1.0 / (1.0 + r)
    p2 = r / (1.0 + r)
    return jnp.where(lane == i1, gp * p1, jnp.where(lane == i2, gp * p2, 0.0))


OUT_SUB = 256


def _outproj_kernel(alpha, x_ref, ya_ref, yb_ref, yc_ref, wo_ref, g_ref, b_ref, wr_ref, br_ref,
                    x1_ref, cmb_ref):
    tm = x_ref.shape[0]
    for r in range(tm // OUT_SUB):
        rows = slice(r * OUT_SUB, (r + 1) * OUT_SUB)
        mix = (_dot(ya_ref[rows, :], wo_ref[0:MLA_OUT, :])
               + _dot(yb_ref[rows, :], wo_ref[MLA_OUT:MLA_OUT + MLSTM_W, :])
               + _dot(yc_ref[rows, :], wo_ref[MLA_OUT + MLSTM_W:, :]))
        x1 = _layernorm(alpha * x_ref[rows, :] + mix, g_ref[...], b_ref[...])
        x1_ref[rows, :] = x1
        x_hi = x1.astype(BF16)
        x_lo = (x1 - x_hi.astype(F32)).astype(BF16)
        a = _dot(x_hi, wr_ref[...])
        c = _dot(x_lo, wr_ref[...])
        logits = (a[:, 0:LANES] + (a[:, LANES:] + c[:, 0:LANES]) + c[:, LANES:]) + br_ref[...]
        cmb_ref[rows, :] = _route(logits)


def _outproj(alpha, x2, ya, yb, yc, wo, g, b, wr, br, tm=1024):
    T, D = x2.shape
    row = lambda w: pl.BlockSpec((tm, w), lambda i: (i, 0))
    full = lambda a: pl.BlockSpec(a.shape, lambda i: (0, 0))
    return pl.pallas_call(
        functools.partial(_outproj_kernel, alpha),
        grid=(T // tm,),
        in_specs=[row(D), row(MLA_OUT), row(MLSTM_W), row(SSD_INNER), full(wo), full(g), full(b),
                  full(wr), full(br)],
        out_specs=[row(D), row(LANES)],
        out_shape=[jax.ShapeDtypeStruct((T, D), F32), jax.ShapeDtypeStruct((T, LANES), F32)],
        compiler_params=_cparams("parallel"),
        name="outproj_ln_router",
    )(x2, ya, yb, yc, wo, g, b, wr, br)


MOE_SUB = 512


def _moe_kernel(alpha, x_ref, cmb_ref, wg_ref, wu_ref, wd_ref, g_ref, b_ref, o_ref, xb_ref, acc_ref):
    grp = pl.program_id(1)

    @pl.when(grp == 0)
    def _():
        xb_ref[...] = x_ref[...].astype(BF16)
        acc_ref[...] = jnp.zeros(acc_ref.shape, F32)

    lane = lax.broadcasted_iota(jnp.int32, (MOE_SUB, LANES), 1)
    for r in range(x_ref.shape[0] // MOE_SUB):
        rows = slice(r * MOE_SUB, (r + 1) * MOE_SUB)
        xb = xb_ref[rows, :]
        cmb = cmb_ref[rows, :]
        hs = []
        for e in range(EXPERTS_PER_GROUP):
            col = N_GROUPS + grp * EXPERTS_PER_GROUP + e
            c = jnp.sum(jnp.where(lane == col, cmb, 0.0), -1, keepdims=True)
            hs.append((_silu(_dot(xb, wg_ref[e])) * _dot(xb, wu_ref[e]) * c).astype(BF16))
        acc_ref[rows, :] += _dot(jnp.concatenate(hs, axis=1), wd_ref[0])

    @pl.when(grp == N_GROUPS - 1)
    def _():
        o_ref[...] = _layernorm(alpha * x_ref[...] + acc_ref[...], g_ref[...], b_ref[...])


def _moe(alpha, x1, cmb, wg, wu, wd, g, b, tm=1024):
    T, D = x1.shape
    F = wg.shape[-1]
    E = EXPERTS_PER_GROUP
    return pl.pallas_call(
        functools.partial(_moe_kernel, alpha),
        grid=(T // tm, N_GROUPS),
        in_specs=[pl.BlockSpec((tm, D), lambda i, e: (i, 0)),
                  pl.BlockSpec((tm, LANES), lambda i, e: (i, 0)),
                  pl.BlockSpec((E, D, F), lambda i, e: (e, 0, 0)),
                  pl.BlockSpec((E, D, F), lambda i, e: (e, 0, 0)),
                  pl.BlockSpec((1, E * F, D), lambda i, e: (e, 0, 0)),
                  pl.BlockSpec((1, D), lambda i, e: (0, 0)),
                  pl.BlockSpec((1, D), lambda i, e: (0, 0))],
        out_specs=pl.BlockSpec((tm, D), lambda i, e: (i, 0)),
        out_shape=jax.ShapeDtypeStruct((T, D), F32),
        scratch_shapes=[pltpu.VMEM((tm, D), BF16), pltpu.VMEM((tm, D), F32)],
        compiler_params=_cparams("parallel", "arbitrary"),
        name="moe_ln",
    )(x1, cmb, wg, wu, wd, g, b)


def _pad_lanes(a, width=LANES):
    return jnp.pad(a, [(0, 0)] * (a.ndim - 1) + [(0, width - a.shape[-1])])


def kernel(x, positions, w_in, mla_q_norm, mla_kv_norm, mla_w_uq, mla_w_ukv, mlstm_gate_bias, mlstm_norm,
           ssd_conv_w, ssd_conv_b, ssd_dt_bias, ssd_a_log, ssd_d, ssd_norm, w_out, ln1_g, ln1_b,
           router_group_w, router_group_b, router_expert_w, router_expert_b,
           expert_w_gate, expert_w_up, expert_w_down, ln2_g, ln2_b):
    B, S, D = x.shape
    depth = w_in.shape[0]
    T = B * S
    alpha = (2 * depth) ** 0.25

    inv = ROPE_BASE ** (-jnp.arange(0, MLA_ROPE, 2, dtype=F32) / MLA_ROPE)
    ang = positions.astype(F32).reshape(T, 1) * inv
    cos, sin = jnp.cos(ang), jnp.sin(ang)
    ones = jnp.ones((T, MLA_NOPE), F32)
    zeros = jnp.zeros((T, MLA_NOPE), F32)
    pad = jnp.zeros((T, LANES - MLA_NOPE - MLA_ROPE), F32)
    cos_t = jnp.concatenate([ones, cos, cos, pad], -1)
    sin_t = jnp.concatenate([zeros, -sin, sin, pad], -1)

    o = [0]
    for s in (MLA_Q_RANK, MLA_KV_RANK, MLA_ROPE, MLSTM_W, MLSTM_W, MLSTM_W, MLSTM_W, 4 * MLSTM_HEADS,
              SSD_INNER, SSD_CONV_DIM, 2 * SSD_HEADS):
        o.append(o[-1] + s)
    zc = lambda n: jnp.zeros((depth, D, n), w_in.dtype)
    w_all = jnp.concatenate(
        [w_in[..., o[0]:o[2]], zc(MLA_NOPE), w_in[..., o[2]:o[3]], zc(LANES - MLA_NOPE - MLA_ROPE),
         w_in[..., o[3]:o[4]], w_in[..., o[4]:o[5]] * (MLSTM_DH ** -0.5), w_in[..., o[5]:o[6]],
         w_in[..., o[6]:o[8]], zc(LANES - 4 * MLSTM_HEADS),
         w_in[..., o[8]:o[11]], zc(LANES - 2 * SSD_HEADS)], -1).astype(BF16)
    head_of_lane = jnp.arange(LANES) // MLSTM_DH
    avg_blk = (head_of_lane[:, None] == head_of_lane[None, :]).astype(BF16) * (1.0 / MLSTM_DH)
    ml_avg = jnp.concatenate([avg_blk, avg_blk], 0)

    wq = mla_w_uq.reshape(depth, MLA_Q_RANK, MLA_HEADS, MLA_NOPE + MLA_ROPE)
    wq = _pad_lanes(wq).reshape(depth, MLA_Q_RANK, MLA_HEADS * LANES).astype(BF16)
    wkv = mla_w_ukv.reshape(depth, MLA_KV_RANK, MLA_HEADS, MLA_NOPE + MLA_V)
    wk = _pad_lanes(wkv[..., :MLA_NOPE]).reshape(depth, MLA_KV_RANK, MLA_HEADS * LANES).astype(BF16)
    wv = wkv[..., MLA_NOPE:].reshape(depth, MLA_KV_RANK, MLA_HEADS // 2, 2 * MLA_V)
    wv = _pad_lanes(wv, 2 * LANES).reshape(depth, MLA_KV_RANK, MLA_HEADS * LANES).astype(BF16)

    gate_bias = _pad_lanes(mlstm_gate_bias.reshape(depth, 1, 4 * MLSTM_HEADS))
    dt_bias = _pad_lanes(ssd_dt_bias.reshape(depth, 1, 2 * SSD_HEADS))
    a_log = _pad_lanes(ssd_a_log.reshape(depth, 1, 2 * SSD_HEADS))
    d_skip = jnp.repeat(ssd_d, SSD_HEADDIM, axis=-1).reshape(depth, 1, SSD_INNER)
    conv_w = jnp.pad(ssd_conv_w, ((0, 0), (0, 8 - SSD_CONV), (0, 0)))

    w_o = w_out.astype(BF16)
    w_r = _pad_lanes(jnp.concatenate(
        [router_group_w, router_expert_w.transpose(0, 2, 1, 3).reshape(depth, D, N_EXPERTS)], -1))
    w_r_hi = w_r.astype(BF16)
    w_r2 = jnp.concatenate([w_r_hi, (w_r - w_r_hi.astype(F32)).astype(BF16)], -1)
    b_r = _pad_lanes(jnp.concatenate(
        [router_group_b, router_expert_b.reshape(depth, N_EXPERTS)], -1).reshape(depth, 1, -1))
    e_g = expert_w_gate.astype(BF16)
    e_u = expert_w_up.astype(BF16)
    e_d = expert_w_down.reshape(depth, N_GROUPS, -1, D).astype(BF16)
    row = lambda a, l: a[l].reshape(1, -1)

    x2 = x.reshape(T, D)
    for l in range(depth):
        p_mla, p_qkv, p_og, p_ssd = _inproj(x2, w_all[l])
        q, k, v = _mla_prep(p_mla, cos_t, sin_t, row(mla_q_norm, l), row(mla_kv_norm, l), wq[l], wk[l], wv[l])
        y_a = _attention(q.reshape(B, S, -1), k.reshape(B, S, -1), v.reshape(B, S, -1))
        y_b = _mlstm(p_qkv.reshape(B, S, -1), p_og.reshape(B, S, -1), gate_bias[l], row(mlstm_norm, l), ml_avg)
        y_c = _ssd(p_ssd.reshape(B, S, -1), conv_w[l], row(ssd_conv_b, l), dt_bias[l], a_log[l],
                   d_skip[l], row(ssd_norm, l))
        x1, cmb = _outproj(alpha, x2, y_a.reshape(T, -1), y_b.reshape(T, -1), y_c.reshape(T, -1),
                           w_o[l], row(ln1_g, l), row(ln1_b, l), w_r2[l], b_r[l])
        x2 = _moe(alpha, x1, cmb, e_g[l], e_u[l], e_d[l], row(ln2_g, l), row(ln2_b, l))
    return x2.reshape(B, S, D)
```

```python
import functools

import jax
import jax.numpy as jnp
from jax import lax
from jax.experimental import pallas as pl
from jax.experimental.pallas import tpu as pltpu

F32 = jnp.float32
BF16 = jnp.bfloat16

LANES = 128
VMEM_LIMIT = 56 * 1024 * 1024

MLA_HEADS = 8
MLA_NOPE = 64
MLA_ROPE = 32
MLA_V = 64
MLA_Q_RANK = 256
MLA_KV_RANK = 128
ROPE_BASE = 10000.0
MLSTM_HEADS = 4
MLSTM_DH = 64
SSD_HEADS = 4
SSD_HEADDIM = 64
SSD_GROUPS = 2
SSD_STATE = 128
SSD_CONV = 5
N_GROUPS = 4
EXPERTS_PER_GROUP = 4
N_EXPERTS = 16
M_INIT = -1e30
LOG2E = 1.4426950408889634

MLA_OUT = MLA_HEADS * MLA_V
MLSTM_W = MLSTM_HEADS * MLSTM_DH
SSD_INNER = SSD_HEADS * SSD_HEADDIM
SSD_CONV_DIM = SSD_INNER + 2 * SSD_GROUPS * SSD_STATE
CHUNK = 128

P_MLA_W = 512
P_SSD_W = SSD_INNER + SSD_CONV_DIM + LANES


def _cparams(*sem):
    return pltpu.CompilerParams(dimension_semantics=sem, vmem_limit_bytes=VMEM_LIMIT)


def _dot(a, b):
    return jnp.dot(a, b, preferred_element_type=F32)


def _dot_nt(a, b):
    return lax.dot_general(a, b, (((1,), (1,)), ((), ())), preferred_element_type=F32)


def _dot_tn(a, b):
    return lax.dot_general(a, b, (((0,), (0,)), ((), ())), preferred_element_type=F32)


def _split3(x):
    hi = x.astype(BF16)
    r = x - hi.astype(F32)
    mid = r.astype(BF16)
    lo = (r - mid.astype(F32)).astype(BF16)
    return hi, mid, lo


def _cumsum_rows(tri_b16, x):
    hi, mid, lo = _split3(x)
    c = _dot(tri_b16, jnp.concatenate([hi, mid, lo], axis=1))
    return c[:, 2 * LANES:3 * LANES] + c[:, LANES:2 * LANES] + c[:, 0:LANES]


def _softplus(x):
    return jnp.maximum(x, 0.0) + jnp.log1p(jnp.exp(-jnp.abs(x)))


def _sigmoid(x):
    return 1.0 / (1.0 + jnp.exp(-x))


def _silu(x):
    return x * _sigmoid(x)


def _inproj_kernel(x_ref, wa_ref, wq_ref, wg_ref, wc_ref, pa_ref, pq_ref, pg_ref, pc_ref):
    x = x_ref[...].astype(BF16)
    pa_ref[...] = _dot(x, wa_ref[...])
    pq_ref[...] = _dot(x, wq_ref[...]).astype(pq_ref.dtype)
    pg_ref[...] = _dot(x, wg_ref[...])
    pc_ref[...] = _dot(x, wc_ref[...])


def _inproj(x2, wa, wq, wg, wc, tm=512):
    T, D = x2.shape
    row = lambda w: pl.BlockSpec((tm, w), lambda i: (i, 0))
    full = lambda a: pl.BlockSpec(a.shape, lambda i: (0, 0))
    return pl.pallas_call(
        _inproj_kernel,
        grid=(T // tm,),
        in_specs=[row(D), full(wa), full(wq), full(wg), full(wc)],
        out_specs=[row(P_MLA_W), row(3 * MLSTM_W), row(MLSTM_W + LANES), row(P_SSD_W)],
        out_shape=[jax.ShapeDtypeStruct((T, P_MLA_W), F32),
                   jax.ShapeDtypeStruct((T, 3 * MLSTM_W), BF16),
                   jax.ShapeDtypeStruct((T, MLSTM_W + LANES), F32),
                   jax.ShapeDtypeStruct((T, P_SSD_W), F32)],
        compiler_params=_cparams("parallel"),
        name="inproj",
    )(x2, wa, wq, wg, wc)


def _mla_prep_kernel(p_ref, cos_ref, sin_ref, qn_ref, kvn_ref, wq_ref, wk_ref, wv_ref,
                     q_ref, k_ref, v_ref):
    p = p_ref[...]
    cq = p[:, 0:MLA_Q_RANK]
    ckv = p[:, MLA_Q_RANK:MLA_Q_RANK + MLA_KV_RANK]
    krb = p[:, MLA_Q_RANK + MLA_KV_RANK:P_MLA_W]
    cqn = (cq * lax.rsqrt(jnp.mean(cq * cq, -1, keepdims=True) + 1e-6) * qn_ref[...]).astype(BF16)
    ckn = (ckv * lax.rsqrt(jnp.mean(ckv * ckv, -1, keepdims=True) + 1e-6) * kvn_ref[...]).astype(BF16)
    q = _dot(cqn, wq_ref[...])
    k = _dot(ckn, wk_ref[...])
    v = _dot(ckn, wv_ref[...])
    vlane = lax.broadcasted_iota(jnp.int32, v.shape, 1)
    v_ref[...] = jnp.where(vlane % (2 * LANES) >= LANES, 1.0, v).astype(v_ref.dtype)
    cos = cos_ref[...]
    sin = sin_ref[...]
    lane = lax.broadcasted_iota(jnp.int32, cos.shape, 1)
    first_half = lane < MLA_NOPE + MLA_ROPE // 2

    def rope(t):
        partner = jnp.where(first_half, pltpu.roll(t, LANES - MLA_ROPE // 2, 1),
                            pltpu.roll(t, MLA_ROPE // 2, 1))
        return t * cos + partner * sin

    scale = (MLA_NOPE + MLA_ROPE) ** -0.5 * LOG2E
    kr = rope(krb)
    for h in range(MLA_HEADS):
        sl = slice(h * LANES, (h + 1) * LANES)
        q_ref[:, sl] = (rope(q[:, sl]) * scale).astype(q_ref.dtype)
        k_ref[:, sl] = (k[:, sl] + kr).astype(k_ref.dtype)


def _mla_prep(p_mla, cos, sin, qn, kvn, wq, wk, wv, tm=512):
    T = p_mla.shape[0]
    row = lambda w: pl.BlockSpec((tm, w), lambda i: (i, 0))
    full = lambda a: pl.BlockSpec(a.shape, lambda i: (0, 0))
    HW = MLA_HEADS * LANES
    return pl.pallas_call(
        _mla_prep_kernel,
        grid=(T // tm,),
        in_specs=[row(P_MLA_W), row(LANES), row(LANES), full(qn), full(kvn), full(wq), full(wk), full(wv)],
        out_specs=[row(HW), row(HW), row(HW)],
        out_shape=[jax.ShapeDtypeStruct((T, HW), BF16)] * 3,
        compiler_params=_cparams("parallel"),
        name="mla_prep",
    )(p_mla, cos, sin, qn, kvn, wq, wk, wv)


ATTN_SUB = 128


def _attn_kernel(q_ref, k_ref, v_ref, o_ref):
    tq = q_ref.shape[1]
    lane = lax.broadcasted_iota(jnp.int32, (ATTN_SUB, LANES), 1)
    for r in range(tq // ATTN_SUB):
        rows = slice(r * ATTN_SUB, (r + 1) * ATTN_SUB)
        outs = []
        for hh in range(2):
            sl = slice(hh * LANES, (hh + 1) * LANES)
            s = _dot_nt(q_ref[0, rows, sl], k_ref[0, :, sl])
            p = jnp.exp2(s - jnp.max(s, -1, keepdims=True)).astype(BF16)
            ov = _dot(p, v_ref[0])
            outs.append(ov[:, 0:LANES] / ov[:, LANES:])
        o_ref[0, rows, :] = jnp.where(lane < MLA_V, outs[0], outs[1]).astype(o_ref.dtype)


def _attention(q, k, v):
    B, S, _ = q.shape
    tq = S
    return pl.pallas_call(
        _attn_kernel,
        grid=(B, MLA_HEADS // 2, S // tq),
        in_specs=[pl.BlockSpec((1, tq, 2 * LANES), lambda b, j, i: (b, i, j)),
                  pl.BlockSpec((1, S, 2 * LANES), lambda b, j, i: (b, 0, j)),
                  pl.BlockSpec((1, S, 2 * LANES), lambda b, j, i: (b, 0, j))],
        out_specs=pl.BlockSpec((1, tq, LANES), lambda b, j, i: (b, i, j)),
        out_shape=jax.ShapeDtypeStruct((B, S, MLA_OUT), BF16),
        compiler_params=_cparams("parallel", "parallel", "arbitrary"),
        name="mla_attn",
    )(q, k, v)


_ML_FCOL = lambda d, h: 8 * d + MLSTM_HEADS + h
_ML_ICOL = lambda d, h: 8 * d + h


def _scan_max_rows(x, row, bwd_lanes):
    L = x.shape[0]
    neg = -jnp.inf
    xf, xb = x, x
    k = 1
    while k < L:
        xf = jnp.maximum(xf, jnp.where(row >= k, pltpu.roll(xf, k, 0), neg))
        xb = jnp.maximum(xb, jnp.where(row < L - k, pltpu.roll(xb, L - k, 0), neg))
        k *= 2
    return jnp.where(bwd_lanes, xb, xf)


def _mlstm_kernel(qkv_ref, og_ref, gb_ref, ng_ref, avg_ref, y_ref,
                  u_s, b_s, tot_s, mdec_s, mpf_s, mpb_s, ds_s, sp_s, st_s):
    S = qkv_ref.shape[1]
    L = CHUNK
    nc = S // L
    H = MLSTM_HEADS
    SW = 2 * LANES
    ti = lax.broadcasted_iota(jnp.int32, (L, L), 0)
    si = lax.broadcasted_iota(jnp.int32, (L, L), 1)
    tri = (si <= ti, si >= ti)
    tri_b16 = tri[0].astype(BF16)
    lane = lax.broadcasted_iota(jnp.int32, (L, LANES), 1)
    row = lax.broadcasted_iota(jnp.int32, (L, LANES), 0)
    lane1 = lax.broadcasted_iota(jnp.int32, (1, LANES), 1)
    bwd_lanes = lane >= 2 * H
    bwd_lanes1 = lane1 >= 2 * H
    lo = lane < MLSTM_DH
    gbias = gb_ref[...]
    ones_b16 = jnp.ones((L, LANES), BF16)
    zero_b16 = jnp.zeros((L, LANES), BF16)
    QO, KO, VO = 0, MLSTM_W, 2 * MLSTM_W
    OO, GO = 0, MLSTM_W

    def head_slices(rows, j, e, off):
        sel = lo if e == 0 else ~lo
        return jnp.where(sel, qkv_ref[0, rows, off + j * LANES:off + (j + 1) * LANES], zero_b16)

    def phase_a(c, carry):
        rows = pl.ds(pl.multiple_of(c * L, L), L)
        G = og_ref[0, rows, GO:GO + LANES] + gbias
        LF = -_softplus(-G)
        pre = _cumsum_rows(tri_b16, LF)
        tot = pre[L - 1:L, :]
        Bm = jnp.where(bwd_lanes, tot - pre + LF, pre)
        U = pltpu.roll(G, H, 1) - Bm
        u_s[rows, :] = U
        b_s[rows, :] = Bm
        dec = tot + U
        mdec = jnp.max(dec, 0, keepdims=True)
        W = jnp.exp(dec - mdec)
        tot_s[c] = tot
        mdec_s[c] = mdec
        for j in range(H // 2):
            for e in range(2):
                h = 2 * j + e
                km = head_slices(rows, j, e, KO)
                vm = head_slices(rows, j, e, VO).astype(F32)
                parts = []
                for d in range(2):
                    wb = jnp.broadcast_to(W[:, _ML_FCOL(d, h):_ML_FCOL(d, h) + 1], (L, LANES))
                    parts += [wb * vm, wb]
                wv = jnp.concatenate(parts, axis=1).astype(BF16)
                ds_s[c * H + h] = _dot_tn(km, wv)
        return carry

    lax.fori_loop(0, nc, phase_a, 0, unroll=4)

    st_s[...] = jnp.zeros(st_s.shape, F32)

    def phase_b(i, m):
        cf, cb = i, nc - 1 - i
        totm = jnp.where(bwd_lanes1, tot_s[cb], tot_s[cf])
        mdecm = jnp.where(bwd_lanes1, mdec_s[cb], mdec_s[cf])
        mpf_s[cf] = m
        mpb_s[cb] = m
        m_new = jnp.maximum(totm + m, mdecm)
        w_c = jnp.exp(totm + m - m_new)
        w_d = jnp.exp(mdecm - m_new)
        for h in range(H):
            for d, c in ((0, cf), (1, cb)):
                part = slice(d * SW, (d + 1) * SW)
                col = slice(_ML_FCOL(d, h), _ML_FCOL(d, h) + 1)
                st = st_s[h, :, part]
                sp_s[c * H + h, :, part] = st.astype(BF16)
                st_s[h, :, part] = w_c[:, col] * st + w_d[:, col] * ds_s[c * H + h, :, part]
        return m_new

    lax.fori_loop(0, nc, phase_b, jnp.full((1, LANES), M_INIT, F32))

    ng = ng_ref[...]
    avg = avg_ref[...]

    def head_mean(t):
        hi = t.astype(BF16)
        lo_part = (t - hi.astype(F32)).astype(BF16)
        return _dot(jnp.concatenate([hi, lo_part], axis=1), avg)

    def phase_c(c, carry):
        rows = pl.ds(pl.multiple_of(c * L, L), L)
        U = u_s[rows, :]
        Bm = b_s[rows, :]
        UT = U.T
        m_prev = jnp.where(bwd_lanes1, mpb_s[c], mpf_s[c])
        Gm = Bm + m_prev
        Mt = jnp.maximum(Gm, Bm + _scan_max_rows(U, row, bwd_lanes))
        Z = Bm - Mt
        WI = jnp.exp(Gm - Mt)
        FL = jnp.exp(-Mt)
        for j in range(H // 2):
            pair = slice(j * LANES, (j + 1) * LANES)
            kb = qkv_ref[0, rows, KO + j * LANES:KO + (j + 1) * LANES]
            hsum = []
            for e in range(2):
                h = 2 * j + e
                qm = head_slices(rows, j, e, QO)
                vaug = jnp.concatenate([head_slices(rows, j, e, VO), ones_b16], axis=1)
                qk = _dot_nt(qm, kb)
                inter = _dot(qm, sp_s[c * H + h])
                ps = []
                for d in range(2):
                    fc = _ML_FCOL(d, h)
                    e_ts = jnp.where(tri[d], UT[fc:fc + 1, :], -jnp.inf) + Z[:, fc:fc + 1]
                    ps.append((qk * jnp.exp(e_ts)).astype(BF16))
                pv = _dot(jnp.concatenate(ps, axis=0), vaug)
                hs = None
                for d in range(2):
                    fc = _ML_FCOL(d, h)
                    nd = pv[d * L:(d + 1) * L] + WI[:, fc:fc + 1] * inter[:, d * SW:(d + 1) * SW]
                    hd = nd[:, 0:LANES] / jnp.maximum(jnp.abs(nd[:, LANES:SW]), FL[:, fc:fc + 1])
                    hs = hd if hs is None else hs + hd
                hsum.append(hs)
            hp = jnp.where(lo, hsum[0], hsum[1])
            dlt = hp - head_mean(hp)
            var = head_mean(dlt * dlt)
            hn = dlt * lax.rsqrt(var + 1e-5) * ng[:, pair]
            o = og_ref[0, rows, OO + j * LANES:OO + (j + 1) * LANES]
            y_ref[0, rows, pair] = (_sigmoid(o) * hn).astype(y_ref.dtype)
        return carry

    lax.fori_loop(0, nc, phase_c, 0, unroll=4)


def _mlstm(p_qkv, p_og, gate_bias, norm_g, avg):
    B, S, _ = p_qkv.shape
    nc = S // CHUNK
    H = MLSTM_HEADS
    vec = lambda: pltpu.VMEM((nc, 1, LANES), F32)
    return pl.pallas_call(
        _mlstm_kernel,
        grid=(B,),
        in_specs=[pl.BlockSpec((1, S, 3 * MLSTM_W), lambda b: (b, 0, 0)),
                  pl.BlockSpec((1, S, MLSTM_W + LANES), lambda b: (b, 0, 0)),
                  pl.BlockSpec((1, LANES), lambda b: (0, 0)),
                  pl.BlockSpec((1, MLSTM_W), lambda b: (0, 0)),
                  pl.BlockSpec(avg.shape, lambda b: (0, 0))],
        out_specs=pl.BlockSpec((1, S, MLSTM_W), lambda b: (b, 0, 0)),
        out_shape=jax.ShapeDtypeStruct((B, S, MLSTM_W), BF16),
        scratch_shapes=[pltpu.VMEM((S, LANES), F32), pltpu.VMEM((S, LANES), F32),
                        vec(), vec(), vec(), vec(),
                        pltpu.VMEM((nc * H, LANES, 4 * LANES), F32),
                        pltpu.VMEM((nc * H, LANES, 4 * LANES), BF16),
                        pltpu.VMEM((H, LANES, 4 * LANES), F32)],
        compiler_params=_cparams("parallel"),
        name="mlstm",
    )(p_qkv, p_og, gate_bias, norm_g, avg)


def _ssd_kernel(p_ref, cw_ref, cb_ref, dtb_ref, alog_ref, dsk_ref, ng_ref, y_ref,
                xpad_ref, xc_ref, cs_s, dt_s, tot_s, ds_s, sp_s, st_ref):
    S = p_ref.shape[1]
    L = CHUNK
    nc = S // L
    PAD = 8
    ZO, XO, DO = 0, SSD_INNER, SSD_INNER + SSD_CONV_DIM

    zpad = jnp.zeros((PAD, SSD_CONV_DIM), F32)
    xpad_ref[0:PAD, :] = zpad
    xpad_ref[PAD + S:PAD + S + PAD, :] = zpad
    R = 256
    for r in range(S // R):
        xpad_ref[PAD + r * R:PAD + (r + 1) * R, :] = p_ref[0, r * R:(r + 1) * R, XO:XO + SSD_CONV_DIM]
    half = SSD_CONV // 2
    for r in range(S // R):
        win = xpad_ref[r * R:r * R + R + 2 * PAD, :]
        acc = jnp.zeros((R, SSD_CONV_DIM), F32) + cb_ref[...]
        for kk in range(SSD_CONV):
            sh = (half - kk) % (R + 2 * PAD)
            tap = win if sh == 0 else pltpu.roll(win, sh, 0)
            acc = acc + tap[PAD:PAD + R, :] * cw_ref[kk:kk + 1, :]
        xc_ref[r * R:(r + 1) * R, :] = _silu(acc)

    ti = lax.broadcasted_iota(jnp.int32, (L, L), 0)
    si = lax.broadcasted_iota(jnp.int32, (L, L), 1)
    tri = (si <= ti, si >= ti)
    tri_b16 = tri[0].astype(BF16)
    lane = lax.broadcasted_iota(jnp.int32, (L, LANES), 1)
    lane1 = lax.broadcasted_iota(jnp.int32, (1, LANES), 1)
    lo = lane < SSD_HEADDIM
    lo1 = lane1 < SSD_HEADDIM
    bwd_lanes = lane >= SSD_HEADS
    a_neg = -jnp.exp(alog_ref[...])
    dtb = dtb_ref[...]
    BO, CO = SSD_INNER, SSD_INNER + SSD_GROUPS * SSD_STATE
    NG = SSD_GROUPS
    col_of = lambda d, g, e: d * SSD_HEADS + NG * g + e

    def pair_cols(a, d, g, mask):
        ce = col_of(d, g, 0)
        return jnp.where(mask, a[:, ce:ce + 1], a[:, ce + 1:ce + 2])

    def phase_a(c, carry):
        rows = pl.ds(pl.multiple_of(c * L, L), L)
        dt = _softplus(p_ref[0, rows, DO:DO + LANES] + dtb)
        dta = dt * a_neg
        pre = _cumsum_rows(tri_b16, dta)
        tot = pre[L - 1:L, :]
        cs = jnp.where(bwd_lanes, tot - pre + dta, pre)
        cs_s[rows, :] = cs
        dt_s[rows, :] = dt
        dec = jnp.exp(tot - cs) * dt
        for g in range(NG):
            xp = xc_ref[rows, g * LANES:(g + 1) * LANES]
            Bg = xc_ref[rows, BO + g * LANES:BO + (g + 1) * LANES].astype(BF16)
            xdt = jnp.concatenate([xp * pair_cols(dec, d, g, lo) for d in range(2)], axis=1)
            inc = _dot_tn(Bg, xdt.astype(BF16))
            for d in range(2):
                tot_s[c * 2 * NG + d * NG + g] = pair_cols(tot, d, g, lo1)
                ds_s[c * 2 * NG + d * NG + g] = inc[:, d * LANES:(d + 1) * LANES]
        return carry

    lax.fori_loop(0, nc, phase_a, 0, unroll=8)

    st_ref[...] = jnp.zeros(st_ref.shape, F32)

    def phase_b(i, carry):
        for d in range(2):
            c = i if d == 0 else nc - 1 - i
            for g in range(NG):
                k = d * NG + g
                st = st_ref[k]
                sp_s[c * NG + g, :, d * LANES:(d + 1) * LANES] = st.astype(BF16)
                st_ref[k] = jnp.exp(tot_s[c * 2 * NG + k]) * st + ds_s[c * 2 * NG + k]
        return carry

    lax.fori_loop(0, nc, phase_b, 0)

    dsk = dsk_ref[...]
    ng = ng_ref[...]

    def phase_c(c, carry):
        rows = pl.ds(pl.multiple_of(c * L, L), L)
        cs = cs_s[rows, :]
        cst = cs.T
        dtt = dt_s[rows, :].T
        for g in range(NG):
            pair = slice(g * LANES, (g + 1) * LANES)
            xp = xc_ref[rows, pair]
            xb = xp.astype(BF16)
            Bg = xc_ref[rows, BO + g * LANES:BO + (g + 1) * LANES].astype(BF16)
            Cg = xc_ref[rows, CO + g * LANES:CO + (g + 1) * LANES].astype(BF16)
            CB = _dot_nt(Cg, Bg)
            yi = _dot(Cg, sp_s[c * NG + g])
            bc = [[jnp.broadcast_to(cs[:, col_of(d, g, e):col_of(d, g, e) + 1], (L, LANES))
                   for e in range(2)] for d in range(2)]
            ys = []
            for e in range(2):
                W = None
                for d in range(2):
                    col = col_of(d, g, e)
                    arg = jnp.where(tri[d], bc[d][e] - cst[col:col + 1, :], -jnp.inf)
                    w = jnp.exp(arg) * dtt[col:col + 1, :]
                    W = w if W is None else W + w
                ys.append(_dot((CB * W).astype(BF16), xb))
            y = jnp.where(lo, ys[0], ys[1])
            for d in range(2):
                y = y + yi[:, d * LANES:(d + 1) * LANES] * jnp.exp(jnp.where(lo, bc[d][0], bc[d][1]))
            y = y + xp * dsk[:, pair]
            y = y * _silu(p_ref[0, rows, ZO + g * LANES:ZO + (g + 1) * LANES])
            y = y * lax.rsqrt(jnp.mean(y * y, -1, keepdims=True) + 1e-6) * ng[:, pair]
            y_ref[0, rows, pair] = y.astype(y_ref.dtype)
        return carry

    lax.fori_loop(0, nc, phase_c, 0, unroll=4)


def _ssd(p_ssd, conv_w, conv_b, dt_bias, a_log, d_skip, norm_g):
    B, S, _ = p_ssd.shape
    nc = S // CHUNK
    NG = SSD_GROUPS
    full = lambda a: pl.BlockSpec(a.shape, lambda b: (0, 0))
    return pl.pallas_call(
        _ssd_kernel,
        grid=(B,),
        in_specs=[pl.BlockSpec((1, S, P_SSD_W), lambda b: (b, 0, 0)),
                  full(conv_w), full(conv_b), full(dt_bias), full(a_log), full(d_skip), full(norm_g)],
        out_specs=pl.BlockSpec((1, S, SSD_INNER), lambda b: (b, 0, 0)),
        out_shape=jax.ShapeDtypeStruct((B, S, SSD_INNER), BF16),
        scratch_shapes=[pltpu.VMEM((S + 16, SSD_CONV_DIM), F32), pltpu.VMEM((S, SSD_CONV_DIM), F32),
                        pltpu.VMEM((S, LANES), F32), pltpu.VMEM((S, LANES), F32),
                        pltpu.VMEM((nc * 2 * NG, 1, LANES), F32),
                        pltpu.VMEM((nc * 2 * NG, SSD_STATE, LANES), F32),
                        pltpu.VMEM((nc * NG, SSD_STATE, 2 * LANES), BF16),
                        pltpu.VMEM((2 * NG, SSD_STATE, LANES), F32)],
        compiler_params=_cparams("parallel"),
        name="ssd",
    )(p_ssd, conv_w, conv_b, dt_bias, a_log, d_skip, norm_g)


def _layernorm(u, g, b):
    mu = jnp.mean(u, -1, keepdims=True)
    d = u - mu
    var = jnp.mean(d * d, -1, keepdims=True)
    return d * lax.rsqrt(var + 1e-5) * g + b


def _route(logits):
    lane_i = lax.broadcasted_iota(jnp.int32, logits.shape, 1)
    lane = lane_i.astype(F32)
    group_of_lane = jnp.right_shift(lane_i - N_GROUPS, 2).astype(F32)
    neg = -jnp.inf
    big = 1e6
    glm = jnp.where(lane_i < N_GROUPS, logits, neg)
    gmax = jnp.max(glm, -1, keepdims=True)
    gp = 1.0 / jnp.sum(jnp.exp(glm - gmax), -1, keepdims=True)
    gi = jnp.min(jnp.where(glm == gmax, lane, big), -1, keepdims=True)
    in_group = (lane_i >= N_GROUPS) & (lane_i < N_GROUPS + N_EXPERTS) & (group_of_lane == gi)
    elm = jnp.where(in_group, logits, neg)
    e1 = jnp.max(elm, -1, keepdims=True)
    i1 = jnp.min(jnp.where(elm == e1, lane, big), -1, keepdims=True)
    elm2 = jnp.where(lane == i1, neg, elm)
    e2 = jnp.max(elm2, -1, keepdims=True)
    i2 = jnp.min(jnp.where(elm2 == e2, lane, big), -1, keepdims=True)
    r = jnp.exp(e2 - e1)
    p1 = 1.0 / (1.0 + r)
    p2 = r / (1.0 + r)
    return jnp.where(lane == i1, gp * p1, jnp.where(lane == i2, gp * p2, 0.0))


OUT_SUB = 256


def _outproj_kernel(alpha, x_ref, ya_ref, yb_ref, yc_ref, wo_ref, g_ref, b_ref, wr_ref, br_ref,
                    x1_ref, cmb_ref, mix0_ref, mix1_ref):
    i = pl.program_id(0)
    tm = x_ref.shape[0]

    @pl.when(i == 0)
    def _():
        mix1_ref[...] = jnp.zeros(mix1_ref.shape, F32)

    def body(mix_w, mix_r):
        for r in range(tm // OUT_SUB):
            rows = slice(r * OUT_SUB, (r + 1) * OUT_SUB)
            mix_w[rows, :] = (_dot(ya_ref[rows, :], wo_ref[0:MLA_OUT, :])
                              + _dot(yb_ref[rows, :], wo_ref[MLA_OUT:MLA_OUT + MLSTM_W, :])
                              + _dot(yc_ref[rows, :], wo_ref[MLA_OUT + MLSTM_W:, :]))
            x1 = _layernorm(alpha * x_ref[rows, :] + mix_r[rows, :], g_ref[...], b_ref[...])
            x1_ref[rows, :] = x1
            x_hi = x1.astype(BF16)
            x_lo = (x1 - x_hi.astype(F32)).astype(BF16)
            a = _dot(x_hi, wr_ref[...])
            c = _dot(x_lo, wr_ref[...])
            logits = (a[:, 0:LANES] + (a[:, LANES:] + c[:, 0:LANES]) + c[:, LANES:]) + br_ref[...]
            cmb_ref[rows, :] = _route(logits)

    @pl.when(i % 2 == 0)
    def _():
        body(mix0_ref, mix1_ref)

    @pl.when(i % 2 == 1)
    def _():
        body(mix1_ref, mix0_ref)


def _outproj(alpha, x2, ya, yb, yc, wo, g, b, wr, br, tm=1024):
    T, D = x2.shape
    n = T // tm
    cur = lambda w: pl.BlockSpec((tm, w), lambda i: (jnp.minimum(i, n - 1), 0))
    prev = lambda w: pl.BlockSpec((tm, w), lambda i: (jnp.maximum(i - 1, 0), 0))
    full = lambda a: pl.BlockSpec(a.shape, lambda i: (0, 0))
    return pl.pallas_call(
        functools.partial(_outproj_kernel, alpha),
        grid=(n + 1,),
        in_specs=[prev(D), cur(MLA_OUT), cur(MLSTM_W), cur(SSD_INNER), full(wo), full(g), full(b),
                  full(wr), full(br)],
        out_specs=[prev(D), prev(LANES)],
        out_shape=[jax.ShapeDtypeStruct((T, D), F32), jax.ShapeDtypeStruct((T, LANES), F32)],
        scratch_shapes=[pltpu.VMEM((tm, D), F32), pltpu.VMEM((tm, D), F32)],
        compiler_params=_cparams("arbitrary"),
        name="outproj_ln_router",
    )(x2, ya, yb, yc, wo, g, b, wr, br)


MOE_SUB = 512


def _moe_kernel(alpha, x_ref, cmb_ref, wg_ref, wu_ref, wd_ref, g_ref, b_ref, o_ref, xb_ref, acc_ref):
    grp = pl.program_id(1)

    @pl.when(grp == 0)
    def _():
        xb_ref[...] = x_ref[...].astype(BF16)
        acc_ref[...] = jnp.zeros(acc_ref.shape, F32)

    lane = lax.broadcasted_iota(jnp.int32, (MOE_SUB, LANES), 1)
    for r in range(x_ref.shape[0] // MOE_SUB):
        rows = slice(r * MOE_SUB, (r + 1) * MOE_SUB)
        xb = xb_ref[rows, :]
        cmb = cmb_ref[rows, :]
        hs = []
        for e in range(EXPERTS_PER_GROUP):
            col = N_GROUPS + grp * EXPERTS_PER_GROUP + e
            c = jnp.sum(jnp.where(lane == col, cmb, 0.0), -1, keepdims=True)
            hs.append((_silu(_dot(xb, wg_ref[e])) * _dot(xb, wu_ref[e]) * c).astype(BF16))
        acc_ref[rows, :] += _dot(jnp.concatenate(hs, axis=1), wd_ref[0])

    @pl.when(grp == N_GROUPS - 1)
    def _():
        o_ref[...] = _layernorm(alpha * x_ref[...] + acc_ref[...], g_ref[...], b_ref[...])


def _moe(alpha, x1, cmb, wg, wu, wd, g, b, tm=1024):
    T, D = x1.shape
    F = wg.shape[-1]
    E = EXPERTS_PER_GROUP
    return pl.pallas_call(
        functools.partial(_moe_kernel, alpha),
        grid=(T // tm, N_GROUPS),
        in_specs=[pl.BlockSpec((tm, D), lambda i, e: (i, 0)),
                  pl.BlockSpec((tm, LANES), lambda i, e: (i, 0)),
                  pl.BlockSpec((E, D, F), lambda i, e: (e, 0, 0)),
                  pl.BlockSpec((E, D, F), lambda i, e: (e, 0, 0)),
                  pl.BlockSpec((1, E * F, D), lambda i, e: (e, 0, 0)),
                  pl.BlockSpec((1, D), lambda i, e: (0, 0)),
                  pl.BlockSpec((1, D), lambda i, e: (0, 0))],
        out_specs=pl.BlockSpec((tm, D), lambda i, e: (i, 0)),
        out_shape=jax.ShapeDtypeStruct((T, D), F32),
        scratch_shapes=[pltpu.VMEM((tm, D), BF16), pltpu.VMEM((tm, D), F32)],
        compiler_params=_cparams("parallel", "arbitrary"),
        name="moe_ln",
    )(x1, cmb, wg, wu, wd, g, b)


def _pad_lanes(a, width=LANES):
    return jnp.pad(a, [(0, 0)] * (a.ndim - 1) + [(0, width - a.shape[-1])])


def kernel(x, positions, w_in, mla_q_norm, mla_kv_norm, mla_w_uq, mla_w_ukv, mlstm_gate_bias, mlstm_norm,
           ssd_conv_w, ssd_conv_b, ssd_dt_bias, ssd_a_log, ssd_d, ssd_norm, w_out, ln1_g, ln1_b,
           router_group_w, router_group_b, router_expert_w, router_expert_b,
           expert_w_gate, expert_w_up, expert_w_down, ln2_g, ln2_b):
    B, S, D = x.shape
    depth = w_in.shape[0]
    T = B * S
    alpha = (2 * depth) ** 0.25

    inv = ROPE_BASE ** (-jnp.arange(0, MLA_ROPE, 2, dtype=F32) / MLA_ROPE)
    ang = positions.astype(F32).reshape(T, 1) * inv
    cos, sin = jnp.cos(ang), jnp.sin(ang)
    ones = jnp.ones((T, MLA_NOPE), F32)
    zeros = jnp.zeros((T, MLA_NOPE), F32)
    pad = jnp.zeros((T, LANES - MLA_NOPE - MLA_ROPE), F32)
    cos_t = jnp.concatenate([ones, cos, cos, pad], -1)
    sin_t = jnp.concatenate([zeros, -sin, sin, pad], -1)

    o = [0]
    for s in (MLA_Q_RANK, MLA_KV_RANK, MLA_ROPE, MLSTM_W, MLSTM_W, MLSTM_W, MLSTM_W, 4 * MLSTM_HEADS,
              SSD_INNER, SSD_CONV_DIM, 2 * SSD_HEADS):
        o.append(o[-1] + s)
    w_b = w_in.astype(BF16)
    zc = lambda n: jnp.zeros((depth, D, n), BF16)
    w_a = jnp.concatenate([w_b[..., o[0]:o[2]], zc(MLA_NOPE), w_b[..., o[2]:o[3]],
                           zc(LANES - MLA_NOPE - MLA_ROPE)], -1)
    w_qkv = jnp.concatenate([w_b[..., o[3]:o[4]], w_b[..., o[4]:o[5]] * (MLSTM_DH ** -0.5),
                             w_b[..., o[5]:o[6]]], -1)
    w_og = jnp.concatenate([w_b[..., o[6]:o[8]], zc(LANES - 4 * MLSTM_HEADS)], -1)
    w_c = jnp.concatenate([w_b[..., o[8]:o[11]], zc(LANES - 2 * SSD_HEADS)], -1)
    head_of_lane = jnp.arange(LANES) // MLSTM_DH
    avg_blk = (head_of_lane[:, None] == head_of_lane[None, :]).astype(BF16) * (1.0 / MLSTM_DH)
    ml_avg = jnp.concatenate([avg_blk, avg_blk], 0)

    wq = mla_w_uq.reshape(depth, MLA_Q_RANK, MLA_HEADS, MLA_NOPE + MLA_ROPE)
    wq = _pad_lanes(wq).reshape(depth, MLA_Q_RANK, MLA_HEADS * LANES).astype(BF16)
    wkv = mla_w_ukv.reshape(depth, MLA_KV_RANK, MLA_HEADS, MLA_NOPE + MLA_V)
    wk = _pad_lanes(wkv[..., :MLA_NOPE]).reshape(depth, MLA_KV_RANK, MLA_HEADS * LANES).astype(BF16)
    wv = wkv[..., MLA_NOPE:].reshape(depth, MLA_KV_RANK, MLA_HEADS // 2, 2 * MLA_V)
    wv = _pad_lanes(wv, 2 * LANES).reshape(depth, MLA_KV_RANK, MLA_HEADS * LANES).astype(BF16)

    gate_bias = _pad_lanes(mlstm_gate_bias.reshape(depth, 1, 4 * MLSTM_HEADS))
    dt_bias = _pad_lanes(ssd_dt_bias.reshape(depth, 1, 2 * SSD_HEADS))
    a_log = _pad_lanes(ssd_a_log.reshape(depth, 1, 2 * SSD_HEADS))
    d_skip = jnp.repeat(ssd_d, SSD_HEADDIM, axis=-1).reshape(depth, 1, SSD_INNER)
    conv_w = jnp.pad(ssd_conv_w, ((0, 0), (0, 8 - SSD_CONV), (0, 0)))

    w_o = w_out.astype(BF16)
    w_r = _pad_lanes(jnp.concatenate(
        [router_group_w, router_expert_w.transpose(0, 2, 1, 3).reshape(depth, D, N_EXPERTS)], -1))
    w_r_hi = w_r.astype(BF16)
    w_r2 = jnp.concatenate([w_r_hi, (w_r - w_r_hi.astype(F32)).astype(BF16)], -1)
    b_r = _pad_lanes(jnp.concatenate(
        [router_group_b, router_expert_b.reshape(depth, N_EXPERTS)], -1).reshape(depth, 1, -1))
    e_g = expert_w_gate.astype(BF16)
    e_u = expert_w_up.astype(BF16)
    e_d = expert_w_down.reshape(depth, N_GROUPS, -1, D).astype(BF16)
    row = lambda a, l: a[l].reshape(1, -1)

    x2 = x.reshape(T, D)
    for l in range(depth):
        p_mla, p_qkv, p_og, p_ssd = _inproj(x2, w_a[l], w_qkv[l], w_og[l], w_c[l])
        q, k, v = _mla_prep(p_mla, cos_t, sin_t, row(mla_q_norm, l), row(mla_kv_norm, l), wq[l], wk[l], wv[l])
        y_a = _attention(q.reshape(B, S, -1), k.reshape(B, S, -1), v.reshape(B, S, -1))
        y_b = _mlstm(p_qkv.reshape(B, S, -1), p_og.reshape(B, S, -1), gate_bias[l], row(mlstm_norm, l), ml_avg)
        y_c = _ssd(p_ssd.reshape(B, S, -1), conv_w[l], row(ssd_conv_b, l), dt_bias[l], a_log[l],
                   d_skip[l], row(ssd_norm, l))
        x1, cmb = _outproj(alpha, x2, y_a.reshape(T, -1), y_b.reshape(T, -1), y_c.reshape(T, -1),
                           w_o[l], row(ln1_g, l), row(ln1_b, l), w_r2[l], b_r[l])
        x2 = _moe(alpha, x1, cmb, e_g[l], e_u[l], e_d[l], row(ln2_g, l), row(ln2_b, l))
    return x2.reshape(B, S, D)
```

```python
import functools

import jax
import jax.numpy as jnp
from jax import lax
from jax.experimental import pallas as pl
from jax.experimental.pallas import tpu as pltpu

F32 = jnp.float32
BF16 = jnp.bfloat16

LANES = 128
VMEM_LIMIT = 56 * 1024 * 1024

MLA_HEADS = 8
MLA_NOPE = 64
MLA_ROPE = 32
MLA_V = 64
MLA_Q_RANK = 256
MLA_KV_RANK = 128
ROPE_BASE = 10000.0
MLSTM_HEADS = 4
MLSTM_DH = 64
SSD_HEADS = 4
SSD_HEADDIM = 64
SSD_GROUPS = 2
SSD_STATE = 128
SSD_CONV = 5
N_GROUPS = 4
EXPERTS_PER_GROUP = 4
N_EXPERTS = 16
M_INIT = -1e30
LOG2E = 1.4426950408889634

MLA_OUT = MLA_HEADS * MLA_V
MLSTM_W = MLSTM_HEADS * MLSTM_DH
SSD_INNER = SSD_HEADS * SSD_HEADDIM
SSD_CONV_DIM = SSD_INNER + 2 * SSD_GROUPS * SSD_STATE
CHUNK = 128

P_MLA_W = 512
P_SSD_W = SSD_INNER + SSD_CONV_DIM + LANES


def _cparams(*sem):
    return pltpu.CompilerParams(dimension_semantics=sem, vmem_limit_bytes=VMEM_LIMIT)


def _dot(a, b):
    return jnp.dot(a, b, preferred_element_type=F32)


def _dot_nt(a, b):
    return lax.dot_general(a, b, (((1,), (1,)), ((), ())), preferred_element_type=F32)


def _dot_tn(a, b):
    return lax.dot_general(a, b, (((0,), (0,)), ((), ())), preferred_element_type=F32)


def _split3(x):
    hi = x.astype(BF16)
    r = x - hi.astype(F32)
    mid = r.astype(BF16)
    lo = (r - mid.astype(F32)).astype(BF16)
    return hi, mid, lo


def _cumsum_rows(tri_b16, x):
    hi, mid, lo = _split3(x)
    c = _dot(tri_b16, jnp.concatenate([hi, mid, lo], axis=1))
    return c[:, 2 * LANES:3 * LANES] + c[:, LANES:2 * LANES] + c[:, 0:LANES]


def _softplus(x):
    return jnp.maximum(x, 0.0) + jnp.log1p(jnp.exp(-jnp.abs(x)))


def _sigmoid(x):
    return 1.0 / (1.0 + jnp.exp(-x))


def _silu(x):
    return x * _sigmoid(x)


def _inproj_kernel(x_ref, wa_ref, wq_ref, wg_ref, wc_ref, pa_ref, pq_ref, pg_ref, pc_ref):
    x = x_ref[...].astype(BF16)
    pa_ref[...] = _dot(x, wa_ref[...])
    pq_ref[...] = _dot(x, wq_ref[...]).astype(pq_ref.dtype)
    pg_ref[...] = _dot(x, wg_ref[...])
    pc_ref[...] = _dot(x, wc_ref[...])


def _inproj(x2, wa, wq, wg, wc, tm=512):
    T, D = x2.shape
    row = lambda w: pl.BlockSpec((tm, w), lambda i: (i, 0))
    full = lambda a: pl.BlockSpec(a.shape, lambda i: (0, 0))
    return pl.pallas_call(
        _inproj_kernel,
        grid=(T // tm,),
        in_specs=[row(D), full(wa), full(wq), full(wg), full(wc)],
        out_specs=[row(P_MLA_W), row(3 * MLSTM_W), row(MLSTM_W + LANES), row(P_SSD_W)],
        out_shape=[jax.ShapeDtypeStruct((T, P_MLA_W), F32),
                   jax.ShapeDtypeStruct((T, 3 * MLSTM_W), BF16),
                   jax.ShapeDtypeStruct((T, MLSTM_W + LANES), F32),
                   jax.ShapeDtypeStruct((T, P_SSD_W), F32)],
        compiler_params=_cparams("parallel"),
        name="inproj",
    )(x2, wa, wq, wg, wc)


def _mla_prep_kernel(p_ref, cos_ref, sin_ref, qn_ref, kvn_ref, wq_ref, wk_ref, wv_ref,
                     q_ref, k_ref, v_ref):
    p = p_ref[...]
    cq = p[:, 0:MLA_Q_RANK]
    ckv = p[:, MLA_Q_RANK:MLA_Q_RANK + MLA_KV_RANK]
    krb = p[:, MLA_Q_RANK + MLA_KV_RANK:P_MLA_W]
    cqn = (cq * lax.rsqrt(jnp.mean(cq * cq, -1, keepdims=True) + 1e-6) * qn_ref[...]).astype(BF16)
    ckn = (ckv * lax.rsqrt(jnp.mean(ckv * ckv, -1, keepdims=True) + 1e-6) * kvn_ref[...]).astype(BF16)
    HW = MLA_HEADS * LANES
    q2 = _dot(cqn, wq_ref[...])
    k = _dot(ckn, wk_ref[...])
    v = _dot(ckn, wv_ref[...])
    vlane = lax.broadcasted_iota(jnp.int32, v.shape, 1)
    v_ref[...] = jnp.where(vlane % (2 * LANES) >= LANES, 1.0, v).astype(v_ref.dtype)
    cos = cos_ref[...]
    sin = sin_ref[...]
    lane = lax.broadcasted_iota(jnp.int32, cos.shape, 1)
    first_half = lane < MLA_NOPE + MLA_ROPE // 2

    def rope(t):
        partner = jnp.where(first_half, pltpu.roll(t, LANES - MLA_ROPE // 2, 1),
                            pltpu.roll(t, MLA_ROPE // 2, 1))
        return t * cos + partner * sin

    scale = (MLA_NOPE + MLA_ROPE) ** -0.5 * LOG2E
    cos_q = cos * scale
    sin_q = sin * scale
    kr = rope(krb)
    for h in range(MLA_HEADS):
        sl = slice(h * LANES, (h + 1) * LANES)
        q_ref[:, sl] = (q2[:, sl] * cos_q + q2[:, HW + h * LANES:HW + (h + 1) * LANES] * sin_q).astype(q_ref.dtype)
        k_ref[:, sl] = (k[:, sl] + kr).astype(k_ref.dtype)


def _mla_prep(p_mla, cos, sin, qn, kvn, wq, wk, wv, tm=1024):
    T = p_mla.shape[0]
    row = lambda w: pl.BlockSpec((tm, w), lambda i: (i, 0))
    full = lambda a: pl.BlockSpec(a.shape, lambda i: (0, 0))
    HW = MLA_HEADS * LANES
    return pl.pallas_call(
        _mla_prep_kernel,
        grid=(T // tm,),
        in_specs=[row(P_MLA_W), row(LANES), row(LANES), full(qn), full(kvn), full(wq), full(wk), full(wv)],
        out_specs=[row(HW), row(HW), row(HW)],
        out_shape=[jax.ShapeDtypeStruct((T, HW), BF16)] * 3,
        compiler_params=_cparams("parallel"),
        name="mla_prep",
    )(p_mla, cos, sin, qn, kvn, wq, wk, wv)


ATTN_SUB = 128


def _attn_kernel(q_ref, k_ref, v_ref, o_ref):
    tq = q_ref.shape[1]
    lane = lax.broadcasted_iota(jnp.int32, (ATTN_SUB, LANES), 1)
    for r in range(tq // ATTN_SUB):
        rows = slice(r * ATTN_SUB, (r + 1) * ATTN_SUB)
        outs = []
        for hh in range(2):
            sl = slice(hh * LANES, (hh + 1) * LANES)
            s = _dot_nt(q_ref[0, rows, sl], k_ref[0, :, sl])
            p = jnp.exp2(s - jnp.max(s, -1, keepdims=True)).astype(BF16)
            ov = _dot(p, v_ref[0])
            outs.append(ov[:, 0:LANES] / ov[:, LANES:])
        o_ref[0, rows, :] = jnp.where(lane < MLA_V, outs[0], outs[1]).astype(o_ref.dtype)


def _attention(q, k, v):
    B, S, _ = q.shape
    tq = S
    return pl.pallas_call(
        _attn_kernel,
        grid=(B, MLA_HEADS // 2, S // tq),
        in_specs=[pl.BlockSpec((1, tq, 2 * LANES), lambda b, j, i: (b, i, j)),
                  pl.BlockSpec((1, S, 2 * LANES), lambda b, j, i: (b, 0, j)),
                  pl.BlockSpec((1, S, 2 * LANES), lambda b, j, i: (b, 0, j))],
        out_specs=pl.BlockSpec((1, tq, LANES), lambda b, j, i: (b, i, j)),
        out_shape=jax.ShapeDtypeStruct((B, S, MLA_OUT), BF16),
        compiler_params=_cparams("parallel", "parallel", "arbitrary"),
        name="mla_attn",
    )(q, k, v)


_ML_FCOL = lambda d, h: 8 * d + MLSTM_HEADS + h
_ML_ICOL = lambda d, h: 8 * d + h


def _scan_max_rows(x, row, bwd_lanes):
    L = x.shape[0]
    neg = -jnp.inf
    xf, xb = x, x
    k = 1
    while k < L:
        xf = jnp.maximum(xf, jnp.where(row >= k, pltpu.roll(xf, k, 0), neg))
        xb = jnp.maximum(xb, jnp.where(row < L - k, pltpu.roll(xb, L - k, 0), neg))
        k *= 2
    return jnp.where(bwd_lanes, xb, xf)


def _mlstm_kernel(qkv_ref, og_ref, gb_ref, ng_ref, avg_ref, y_ref,
                  u_s, b_s, tot_s, mdec_s, mpf_s, mpb_s, ds_s, sp_s, st_s):
    S = qkv_ref.shape[1]
    L = CHUNK
    nc = S // L
    H = MLSTM_HEADS
    SW = 2 * LANES
    ti = lax.broadcasted_iota(jnp.int32, (L, L), 0)
    si = lax.broadcasted_iota(jnp.int32, (L, L), 1)
    tri = (si <= ti, si >= ti)
    tri_b16 = tri[0].astype(BF16)
    lane = lax.broadcasted_iota(jnp.int32, (L, LANES), 1)
    row = lax.broadcasted_iota(jnp.int32, (L, LANES), 0)
    lane1 = lax.broadcasted_iota(jnp.int32, (1, LANES), 1)
    bwd_lanes = lane >= 2 * H
    bwd_lanes1 = lane1 >= 2 * H
    lo = lane < MLSTM_DH
    gbias = gb_ref[...]
    ones_b16 = jnp.ones((L, LANES), BF16)
    zero_b16 = jnp.zeros((L, LANES), BF16)
    QO, KO, VO = 0, MLSTM_W, 2 * MLSTM_W
    OO, GO = 0, MLSTM_W

    def head_slices(rows, j, e, off):
        sel = lo if e == 0 else ~lo
        return jnp.where(sel, qkv_ref[0, rows, off + j * LANES:off + (j + 1) * LANES], zero_b16)

    def phase_a(c, carry):
        rows = pl.ds(pl.multiple_of(c * L, L), L)
        G = og_ref[0, rows, GO:GO + LANES] + gbias
        LF = -_softplus(-G)
        pre = _cumsum_rows(tri_b16, LF)
        tot = pre[L - 1:L, :]
        Bm = jnp.where(bwd_lanes, tot - pre + LF, pre)
        U = pltpu.roll(G, H, 1) - Bm
        u_s[rows, :] = U
        b_s[rows, :] = Bm
        dec = tot + U
        mdec = jnp.max(dec, 0, keepdims=True)
        W = jnp.exp(dec - mdec)
        tot_s[c] = tot
        mdec_s[c] = mdec
        for j in range(H // 2):
            for e in range(2):
                h = 2 * j + e
                km = head_slices(rows, j, e, KO)
                vm = head_slices(rows, j, e, VO).astype(F32)
                parts = []
                for d in range(2):
                    wb = jnp.broadcast_to(W[:, _ML_FCOL(d, h):_ML_FCOL(d, h) + 1], (L, LANES))
                    parts += [wb * vm, wb]
                wv = jnp.concatenate(parts, axis=1).astype(BF16)
                ds_s[c * H + h] = _dot_tn(km, wv)
        return carry

    lax.fori_loop(0, nc, phase_a, 0, unroll=4)

    st_s[...] = jnp.zeros(st_s.shape, F32)

    def phase_b(i, m):
        cf, cb = i, nc - 1 - i
        totm = jnp.where(bwd_lanes1, tot_s[cb], tot_s[cf])
        mdecm = jnp.where(bwd_lanes1, mdec_s[cb], mdec_s[cf])
        mpf_s[cf] = m
        mpb_s[cb] = m
        m_new = jnp.maximum(totm + m, mdecm)
        w_c = jnp.exp(totm + m - m_new)
        w_d = jnp.exp(mdecm - m_new)
        for h in range(H):
            for d, c in ((0, cf), (1, cb)):
                part = slice(d * SW, (d + 1) * SW)
                col = slice(_ML_FCOL(d, h), _ML_FCOL(d, h) + 1)
                st = st_s[h, :, part]
                sp_s[c * H + h, :, part] = st.astype(BF16)
                st_s[h, :, part] = w_c[:, col] * st + w_d[:, col] * ds_s[c * H + h, :, part]
        return m_new

    lax.fori_loop(0, nc, phase_b, jnp.full((1, LANES), M_INIT, F32), unroll=4)

    ng = ng_ref[...]
    avg = avg_ref[...]

    def head_mean(t):
        hi = t.astype(BF16)
        lo_part = (t - hi.astype(F32)).astype(BF16)
        return _dot(jnp.concatenate([hi, lo_part], axis=1), avg)

    def phase_c(c, carry):
        rows = pl.ds(pl.multiple_of(c * L, L), L)
        U = u_s[rows, :]
        Bm = b_s[rows, :]
        UT = U.T
        m_prev = jnp.where(bwd_lanes1, mpb_s[c], mpf_s[c])
        Gm = Bm + m_prev
        Mt = jnp.maximum(Gm, Bm + _scan_max_rows(U, row, bwd_lanes))
        Z = Bm - Mt
        WI = jnp.exp(Gm - Mt)
        FL = jnp.exp(-Mt)
        for j in range(H // 2):
            pair = slice(j * LANES, (j + 1) * LANES)
            kb = qkv_ref[0, rows, KO + j * LANES:KO + (j + 1) * LANES]
            hsum = []
            for e in range(2):
                h = 2 * j + e
                qm = head_slices(rows, j, e, QO)
                vaug = jnp.concatenate([head_slices(rows, j, e, VO), ones_b16], axis=1)
                qk = _dot_nt(qm, kb)
                inter = _dot(qm, sp_s[c * H + h])
                ps = []
                for d in range(2):
                    fc = _ML_FCOL(d, h)
                    e_ts = jnp.where(tri[d], UT[fc:fc + 1, :], -jnp.inf) + Z[:, fc:fc + 1]
                    ps.append((qk * jnp.exp(e_ts)).astype(BF16))
                pv = _dot(jnp.concatenate(ps, axis=0), vaug)
                hs = None
                for d in range(2):
                    fc = _ML_FCOL(d, h)
                    nd = pv[d * L:(d + 1) * L] + WI[:, fc:fc + 1] * inter[:, d * SW:(d + 1) * SW]
                    hd = nd[:, 0:LANES] / jnp.maximum(jnp.abs(nd[:, LANES:SW]), FL[:, fc:fc + 1])
                    hs = hd if hs is None else hs + hd
                hsum.append(hs)
            hp = jnp.where(lo, hsum[0], hsum[1])
            dlt = hp - head_mean(hp)
            var = head_mean(dlt * dlt)
            hn = dlt * lax.rsqrt(var + 1e-5) * ng[:, pair]
            o = og_ref[0, rows, OO + j * LANES:OO + (j + 1) * LANES]
            y_ref[0, rows, pair] = (_sigmoid(o) * hn).astype(y_ref.dtype)
        return carry

    lax.fori_loop(0, nc, phase_c, 0, unroll=4)


def _mlstm(p_qkv, p_og, gate_bias, norm_g, avg):
    B, S, _ = p_qkv.shape
    nc = S // CHUNK
    H = MLSTM_HEADS
    vec = lambda: pltpu.VMEM((nc, 1, LANES), F32)
    return pl.pallas_call(
        _mlstm_kernel,
        grid=(B,),
        in_specs=[pl.BlockSpec((1, S, 3 * MLSTM_W), lambda b: (b, 0, 0)),
                  pl.BlockSpec((1, S, MLSTM_W + LANES), lambda b: (b, 0, 0)),
                  pl.BlockSpec((1, LANES), lambda b: (0, 0)),
                  pl.BlockSpec((1, MLSTM_W), lambda b: (0, 0)),
                  pl.BlockSpec(avg.shape, lambda b: (0, 0))],
        out_specs=pl.BlockSpec((1, S, MLSTM_W), lambda b: (b, 0, 0)),
        out_shape=jax.ShapeDtypeStruct((B, S, MLSTM_W), BF16),
        scratch_shapes=[pltpu.VMEM((S, LANES), F32), pltpu.VMEM((S, LANES), F32),
                        vec(), vec(), vec(), vec(),
                        pltpu.VMEM((nc * H, LANES, 4 * LANES), F32),
                        pltpu.VMEM((nc * H, LANES, 4 * LANES), BF16),
                        pltpu.VMEM((H, LANES, 4 * LANES), F32)],
        compiler_params=_cparams("parallel"),
        name="mlstm",
    )(p_qkv, p_og, gate_bias, norm_g, avg)


def _ssd_kernel(p_ref, cw_ref, cb_ref, dtb_ref, alog_ref, dsk_ref, ng_ref, y_ref,
                xpad_ref, xc_ref, cs_s, dt_s, tot_s, ds_s, sp_s, st_ref):
    S = p_ref.shape[1]
    L = CHUNK
    nc = S // L
    PAD = 8
    ZO, XO, DO = 0, SSD_INNER, SSD_INNER + SSD_CONV_DIM

    zpad = jnp.zeros((PAD, SSD_CONV_DIM), F32)
    xpad_ref[0:PAD, :] = zpad
    xpad_ref[PAD + S:PAD + S + PAD, :] = zpad
    R = 256
    for r in range(S // R):
        xpad_ref[PAD + r * R:PAD + (r + 1) * R, :] = p_ref[0, r * R:(r + 1) * R, XO:XO + SSD_CONV_DIM]
    half = SSD_CONV // 2
    for r in range(S // R):
        win = xpad_ref[r * R:r * R + R + 2 * PAD, :]
        acc = jnp.zeros((R, SSD_CONV_DIM), F32) + cb_ref[...]
        for kk in range(SSD_CONV):
            sh = (half - kk) % (R + 2 * PAD)
            tap = win if sh == 0 else pltpu.roll(win, sh, 0)
            acc = acc + tap[PAD:PAD + R, :] * cw_ref[kk:kk + 1, :]
        xc_ref[r * R:(r + 1) * R, :] = _silu(acc)

    ti = lax.broadcasted_iota(jnp.int32, (L, L), 0)
    si = lax.broadcasted_iota(jnp.int32, (L, L), 1)
    tri = (si <= ti, si >= ti)
    tri_b16 = tri[0].astype(BF16)
    lane = lax.broadcasted_iota(jnp.int32, (L, LANES), 1)
    lane1 = lax.broadcasted_iota(jnp.int32, (1, LANES), 1)
    lo = lane < SSD_HEADDIM
    lo1 = lane1 < SSD_HEADDIM
    bwd_lanes = lane >= SSD_HEADS
    a_neg = -jnp.exp(alog_ref[...])
    dtb = dtb_ref[...]
    BO, CO = SSD_INNER, SSD_INNER + SSD_GROUPS * SSD_STATE
    NG = SSD_GROUPS
    col_of = lambda d, g, e: d * SSD_HEADS + NG * g + e

    def pair_cols(a, d, g, mask):
        ce = col_of(d, g, 0)
        return jnp.where(mask, a[:, ce:ce + 1], a[:, ce + 1:ce + 2])

    def phase_a(c, carry):
        rows = pl.ds(pl.multiple_of(c * L, L), L)
        dt = _softplus(p_ref[0, rows, DO:DO + LANES] + dtb)
        dta = dt * a_neg
        pre = _cumsum_rows(tri_b16, dta)
        tot = pre[L - 1:L, :]
        cs = jnp.where(bwd_lanes, tot - pre + dta, pre)
        cs_s[rows, :] = cs
        dt_s[rows, :] = dt
        dec = jnp.exp(tot - cs) * dt
        for g in range(NG):
            xp = xc_ref[rows, g * LANES:(g + 1) * LANES]
            Bg = xc_ref[rows, BO + g * LANES:BO + (g + 1) * LANES].astype(BF16)
            xdt = jnp.concatenate([xp * pair_cols(dec, d, g, lo) for d in range(2)], axis=1)
            inc = _dot_tn(Bg, xdt.astype(BF16))
            for d in range(2):
                tot_s[c * 2 * NG + d * NG + g] = pair_cols(tot, d, g, lo1)
                ds_s[c * 2 * NG + d * NG + g] = inc[:, d * LANES:(d + 1) * LANES]
        return carry

    lax.fori_loop(0, nc, phase_a, 0, unroll=8)

    st_ref[...] = jnp.zeros(st_ref.shape, F32)

    def phase_b(i, carry):
        for d in range(2):
            c = i if d == 0 else nc - 1 - i
            for g in range(NG):
                k = d * NG + g
                st = st_ref[k]
                sp_s[c * NG + g, :, d * LANES:(d + 1) * LANES] = st.astype(BF16)
                st_ref[k] = jnp.exp(tot_s[c * 2 * NG + k]) * st + ds_s[c * 2 * NG + k]
        return carry

    lax.fori_loop(0, nc, phase_b, 0)

    dsk = dsk_ref[...]
    ng = ng_ref[...]

    def phase_c(c, carry):
        rows = pl.ds(pl.multiple_of(c * L, L), L)
        cs = cs_s[rows, :]
        cst = cs.T
        dtt = dt_s[rows, :].T
        for g in range(NG):
            pair = slice(g * LANES, (g + 1) * LANES)
            xp = xc_ref[rows, pair]
            xb = xp.astype(BF16)
            Bg = xc_ref[rows, BO + g * LANES:BO + (g + 1) * LANES].astype(BF16)
            Cg = xc_ref[rows, CO + g * LANES:CO + (g + 1) * LANES].astype(BF16)
            CB = _dot_nt(Cg, Bg)
            yi = _dot(Cg, sp_s[c * NG + g])
            bc = [[jnp.broadcast_to(cs[:, col_of(d, g, e):col_of(d, g, e) + 1], (L, LANES))
                   for e in range(2)] for d in range(2)]
            ys = []
            for e in range(2):
                W = None
                for d in range(2):
                    col = col_of(d, g, e)
                    arg = jnp.where(tri[d], bc[d][e] - cst[col:col + 1, :], -jnp.inf)
                    w = jnp.exp(arg) * dtt[col:col + 1, :]
                    W = w if W is None else W + w
                ys.append(_dot((CB * W).astype(BF16), xb))
            y = jnp.where(lo, ys[0], ys[1])
            for d in range(2):
                y = y + yi[:, d * LANES:(d + 1) * LANES] * jnp.exp(jnp.where(lo, bc[d][0], bc[d][1]))
            y = y + xp * dsk[:, pair]
            y = y * _silu(p_ref[0, rows, ZO + g * LANES:ZO + (g + 1) * LANES])
            y = y * lax.rsqrt(jnp.mean(y * y, -1, keepdims=True) + 1e-6) * ng[:, pair]
            y_ref[0, rows, pair] = y.astype(y_ref.dtype)
        return carry

    lax.fori_loop(0, nc, phase_c, 0, unroll=4)


def _ssd(p_ssd, conv_w, conv_b, dt_bias, a_log, d_skip, norm_g):
    B, S, _ = p_ssd.shape
    nc = S // CHUNK
    NG = SSD_GROUPS
    full = lambda a: pl.BlockSpec(a.shape, lambda b: (0, 0))
    return pl.pallas_call(
        _ssd_kernel,
        grid=(B,),
        in_specs=[pl.BlockSpec((1, S, P_SSD_W), lambda b: (b, 0, 0)),
                  full(conv_w), full(conv_b), full(dt_bias), full(a_log), full(d_skip), full(norm_g)],
        out_specs=pl.BlockSpec((1, S, SSD_INNER), lambda b: (b, 0, 0)),
        out_shape=jax.ShapeDtypeStruct((B, S, SSD_INNER), BF16),
        scratch_shapes=[pltpu.VMEM((S + 16, SSD_CONV_DIM), F32), pltpu.VMEM((S, SSD_CONV_DIM), F32),
                        pltpu.VMEM((S, LANES), F32), pltpu.VMEM((S, LANES), F32),
                        pltpu.VMEM((nc * 2 * NG, 1, LANES), F32),
                        pltpu.VMEM((nc * 2 * NG, SSD_STATE, LANES), F32),
                        pltpu.VMEM((nc * NG, SSD_STATE, 2 * LANES), BF16),
                        pltpu.VMEM((2 * NG, SSD_STATE, LANES), F32)],
        compiler_params=_cparams("parallel"),
        name="ssd",
    )(p_ssd, conv_w, conv_b, dt_bias, a_log, d_skip, norm_g)


def _layernorm(u, g, b):
    mu = jnp.mean(u, -1, keepdims=True)
    d = u - mu
    var = jnp.mean(d * d, -1, keepdims=True)
    return d * lax.rsqrt(var + 1e-5) * g + b


def _route(logits):
    lane_i = lax.broadcasted_iota(jnp.int32, logits.shape, 1)
    lane = lane_i.astype(F32)
    group_of_lane = jnp.right_shift(lane_i - N_GROUPS, 2).astype(F32)
    neg = -jnp.inf
    big = 1e6
    glm = jnp.where(lane_i < N_GROUPS, logits, neg)
    gmax = jnp.max(glm, -1, keepdims=True)
    gp = 1.0 / jnp.sum(jnp.exp(glm - gmax), -1, keepdims=True)
    gi = jnp.min(jnp.where(glm == gmax, lane, big), -1, keepdims=True)
    in_group = (lane_i >= N_GROUPS) & (lane_i < N_GROUPS + N_EXPERTS) & (group_of_lane == gi)
    elm = jnp.where(in_group, logits, neg)
    e1 = jnp.max(elm, -1, keepdims=True)
    i1 = jnp.min(jnp.where(elm == e1, lane, big), -1, keepdims=True)
    elm2 = jnp.where(lane == i1, neg, elm)
    e2 = jnp.max(elm2, -1, keepdims=True)
    i2 = jnp.min(jnp.where(elm2 == e2, lane, big), -1, keepdims=True)
    r = jnp.exp(e2 - e1)
    p1 = 1.0 / (1.0 + r)
    p2 = r / (1.0 + r)
    return jnp.where(lane == i1, gp * p1, jnp.where(lane == i2, gp * p2, 0.0))


OUT_SUB = 256


def _outproj_kernel(alpha, x_ref, ya_ref, yb_ref, yc_ref, wo_ref, g_ref, b_ref, wr_ref, br_ref,
                    x1_ref, cmb_ref, mix0_ref, mix1_ref):
    i = pl.program_id(0)
    tm = x_ref.shape[0]

    @pl.when(i == 0)
    def _():
        mix1_ref[...] = jnp.zeros(mix1_ref.shape, F32)

    def body(mix_w, mix_r):
        for r in range(tm // OUT_SUB):
            rows = slice(r * OUT_SUB, (r + 1) * OUT_SUB)
            mix_w[rows, :] = (_dot(ya_ref[rows, :], wo_ref[0:MLA_OUT, :])
                              + _dot(yb_ref[rows, :], wo_ref[MLA_OUT:MLA_OUT + MLSTM_W, :])
                              + _dot(yc_ref[rows, :], wo_ref[MLA_OUT + MLSTM_W:, :]))
            x1 = _layernorm(alpha * x_ref[rows, :] + mix_r[rows, :], g_ref[...], b_ref[...])
            x1_ref[rows, :] = x1
            x_hi = x1.astype(BF16)
            x_lo = (x1 - x_hi.astype(F32)).astype(BF16)
            a = _dot(x_hi, wr_ref[...])
            c = _dot(x_lo, wr_ref[...])
            logits = (a[:, 0:LANES] + (a[:, LANES:] + c[:, 0:LANES]) + c[:, LANES:]) + br_ref[...]
            cmb_ref[rows, :] = _route(logits)

    @pl.when(i % 2 == 0)
    def _():
        body(mix0_ref, mix1_ref)

    @pl.when(i % 2 == 1)
    def _():
        body(mix1_ref, mix0_ref)


def _outproj(alpha, x2, ya, yb, yc, wo, g, b, wr, br, tm=1024):
    T, D = x2.shape
    n = T // tm
    cur = lambda w: pl.BlockSpec((tm, w), lambda i: (jnp.minimum(i, n - 1), 0))
    prev = lambda w: pl.BlockSpec((tm, w), lambda i: (jnp.maximum(i - 1, 0), 0))
    full = lambda a: pl.BlockSpec(a.shape, lambda i: (0, 0))
    return pl.pallas_call(
        functools.partial(_outproj_kernel, alpha),
        grid=(n + 1,),
        in_specs=[prev(D), cur(MLA_OUT), cur(MLSTM_W), cur(SSD_INNER), full(wo), full(g), full(b),
                  full(wr), full(br)],
        out_specs=[prev(D), prev(LANES)],
        out_shape=[jax.ShapeDtypeStruct((T, D), F32), jax.ShapeDtypeStruct((T, LANES), F32)],
        scratch_shapes=[pltpu.VMEM((tm, D), F32), pltpu.VMEM((tm, D), F32)],
        compiler_params=_cparams("arbitrary"),
        name="outproj_ln_router",
    )(x2, ya, yb, yc, wo, g, b, wr, br)


MOE_SUB = 512


def _moe_kernel(alpha, x_ref, cmb_ref, wg_ref, wu_ref, wd_ref, g_ref, b_ref, o_ref, xb_ref, acc_ref):
    grp = pl.program_id(1)

    @pl.when(grp == 0)
    def _():
        xb_ref[...] = x_ref[...].astype(BF16)
        acc_ref[...] = jnp.zeros(acc_ref.shape, F32)

    lane = lax.broadcasted_iota(jnp.int32, (MOE_SUB, LANES), 1)
    for r in range(x_ref.shape[0] // MOE_SUB):
        rows = slice(r * MOE_SUB, (r + 1) * MOE_SUB)
        xb = xb_ref[rows, :]
        cmb = cmb_ref[rows, :]
        hs = []
        for e in range(EXPERTS_PER_GROUP):
            col = N_GROUPS + grp * EXPERTS_PER_GROUP + e
            c = jnp.sum(jnp.where(lane == col, cmb, 0.0), -1, keepdims=True)
            hs.append((_silu(_dot(xb, wg_ref[e])) * _dot(xb, wu_ref[e]) * c).astype(BF16))
        acc_ref[rows, :] += _dot(jnp.concatenate(hs, axis=1), wd_ref[0])

    @pl.when(grp == N_GROUPS - 1)
    def _():
        o_ref[...] = _layernorm(alpha * x_ref[...] + acc_ref[...], g_ref[...], b_ref[...])


def _moe(alpha, x1, cmb, wg, wu, wd, g, b, tm=1024):
    T, D = x1.shape
    F = wg.shape[-1]
    E = EXPERTS_PER_GROUP
    return pl.pallas_call(
        functools.partial(_moe_kernel, alpha),
        grid=(T // tm, N_GROUPS),
        in_specs=[pl.BlockSpec((tm, D), lambda i, e: (i, 0)),
                  pl.BlockSpec((tm, LANES), lambda i, e: (i, 0)),
                  pl.BlockSpec((E, D, F), lambda i, e: (e, 0, 0)),
                  pl.BlockSpec((E, D, F), lambda i, e: (e, 0, 0)),
                  pl.BlockSpec((1, E * F, D), lambda i, e: (e, 0, 0)),
                  pl.BlockSpec((1, D), lambda i, e: (0, 0)),
                  pl.BlockSpec((1, D), lambda i, e: (0, 0))],
        out_specs=pl.BlockSpec((tm, D), lambda i, e: (i, 0)),
        out_shape=jax.ShapeDtypeStruct((T, D), F32),
        scratch_shapes=[pltpu.VMEM((tm, D), BF16), pltpu.VMEM((tm, D), F32)],
        compiler_params=_cparams("parallel", "arbitrary"),
        name="moe_ln",
    )(x1, cmb, wg, wu, wd, g, b)


def _pad_lanes(a, width=LANES):
    return jnp.pad(a, [(0, 0)] * (a.ndim - 1) + [(0, width - a.shape[-1])])


def kernel(x, positions, w_in, mla_q_norm, mla_kv_norm, mla_w_uq, mla_w_ukv, mlstm_gate_bias, mlstm_norm,
           ssd_conv_w, ssd_conv_b, ssd_dt_bias, ssd_a_log, ssd_d, ssd_norm, w_out, ln1_g, ln1_b,
           router_group_w, router_group_b, router_expert_w, router_expert_b,
           expert_w_gate, expert_w_up, expert_w_down, ln2_g, ln2_b):
    B, S, D = x.shape
    depth = w_in.shape[0]
    T = B * S
    alpha = (2 * depth) ** 0.25

    inv = ROPE_BASE ** (-jnp.arange(0, MLA_ROPE, 2, dtype=F32) / MLA_ROPE)
    ang = positions.astype(F32).reshape(T, 1) * inv
    cos, sin = jnp.cos(ang), jnp.sin(ang)
    ones = jnp.ones((T, MLA_NOPE), F32)
    zeros = jnp.zeros((T, MLA_NOPE), F32)
    pad = jnp.zeros((T, LANES - MLA_NOPE - MLA_ROPE), F32)
    cos_t = jnp.concatenate([ones, cos, cos, pad], -1)
    sin_t = jnp.concatenate([zeros, -sin, sin, pad], -1)

    o = [0]
    for s in (MLA_Q_RANK, MLA_KV_RANK, MLA_ROPE, MLSTM_W, MLSTM_W, MLSTM_W, MLSTM_W, 4 * MLSTM_HEADS,
              SSD_INNER, SSD_CONV_DIM, 2 * SSD_HEADS):
        o.append(o[-1] + s)
    w_b = w_in.astype(BF16)
    zc = lambda n: jnp.zeros((depth, D, n), BF16)
    w_a = jnp.concatenate([w_b[..., o[0]:o[2]], zc(MLA_NOPE), w_b[..., o[2]:o[3]],
                           zc(LANES - MLA_NOPE - MLA_ROPE)], -1)
    w_qkv = jnp.concatenate([w_b[..., o[3]:o[4]], w_b[..., o[4]:o[5]] * (MLSTM_DH ** -0.5),
                             w_b[..., o[5]:o[6]]], -1)
    w_og = jnp.concatenate([w_b[..., o[6]:o[8]], zc(LANES - 4 * MLSTM_HEADS)], -1)
    w_c = jnp.concatenate([w_b[..., o[8]:o[11]], zc(LANES - 2 * SSD_HEADS)], -1)
    head_of_lane = jnp.arange(LANES) // MLSTM_DH
    avg_blk = (head_of_lane[:, None] == head_of_lane[None, :]).astype(BF16) * (1.0 / MLSTM_DH)
    ml_avg = jnp.concatenate([avg_blk, avg_blk], 0)

    wq4 = mla_w_uq.reshape(depth, MLA_Q_RANK, MLA_HEADS, MLA_NOPE + MLA_ROPE)
    half = MLA_ROPE // 2
    wq_swap = jnp.concatenate([jnp.zeros_like(wq4[..., :MLA_NOPE]), wq4[..., MLA_NOPE + half:],
                               wq4[..., MLA_NOPE:MLA_NOPE + half]], -1)
    flat = lambda w: _pad_lanes(w).reshape(depth, MLA_Q_RANK, MLA_HEADS * LANES)
    wq = jnp.concatenate([flat(wq4), flat(wq_swap)], -1).astype(BF16)
    wkv = mla_w_ukv.reshape(depth, MLA_KV_RANK, MLA_HEADS, MLA_NOPE + MLA_V)
    wk = _pad_lanes(wkv[..., :MLA_NOPE]).reshape(depth, MLA_KV_RANK, MLA_HEADS * LANES).astype(BF16)
    wv = wkv[..., MLA_NOPE:].reshape(depth, MLA_KV_RANK, MLA_HEADS // 2, 2 * MLA_V)
    wv = _pad_lanes(wv, 2 * LANES).reshape(depth, MLA_KV_RANK, MLA_HEADS * LANES).astype(BF16)

    gate_bias = _pad_lanes(mlstm_gate_bias.reshape(depth, 1, 4 * MLSTM_HEADS))
    dt_bias = _pad_lanes(ssd_dt_bias.reshape(depth, 1, 2 * SSD_HEADS))
    a_log = _pad_lanes(ssd_a_log.reshape(depth, 1, 2 * SSD_HEADS))
    d_skip = jnp.repeat(ssd_d, SSD_HEADDIM, axis=-1).reshape(depth, 1, SSD_INNER)
    conv_w = jnp.pad(ssd_conv_w, ((0, 0), (0, 8 - SSD_CONV), (0, 0)))

    w_o = w_out.astype(BF16)
    w_r = _pad_lanes(jnp.concatenate(
        [router_group_w, router_expert_w.transpose(0, 2, 1, 3).reshape(depth, D, N_EXPERTS)], -1))
    w_r_hi = w_r.astype(BF16)
    w_r2 = jnp.concatenate([w_r_hi, (w_r - w_r_hi.astype(F32)).astype(BF16)], -1)
    b_r = _pad_lanes(jnp.concatenate(
        [router_group_b, router_expert_b.reshape(depth, N_EXPERTS)], -1).reshape(depth, 1, -1))
    e_g = expert_w_gate.astype(BF16)
    e_u = expert_w_up.astype(BF16)
    e_d = expert_w_down.reshape(depth, N_GROUPS, -1, D).astype(BF16)
    row = lambda a, l: a[l].reshape(1, -1)

    x2 = x.reshape(T, D)
    for l in range(depth):
        p_mla, p_qkv, p_og, p_ssd = _inproj(x2, w_a[l], w_qkv[l], w_og[l], w_c[l])
        q, k, v = _mla_prep(p_mla, cos_t, sin_t, row(mla_q_norm, l), row(mla_kv_norm, l), wq[l], wk[l], wv[l])
        y_a = _attention(q.reshape(B, S, -1), k.reshape(B, S, -1), v.reshape(B, S, -1))
        y_b = _mlstm(p_qkv.reshape(B, S, -1), p_og.reshape(B, S, -1), gate_bias[l], row(mlstm_norm, l), ml_avg)
        y_c = _ssd(p_ssd.reshape(B, S, -1), conv_w[l], row(ssd_conv_b, l), dt_bias[l], a_log[l],
                   d_skip[l], row(ssd_norm, l))
        x1, cmb = _outproj(alpha, x2, y_a.reshape(T, -1), y_b.reshape(T, -1), y_c.reshape(T, -1),
                           w_o[l], row(ln1_g, l), row(ln1_b, l), w_r2[l], b_r[l])
        x2 = _moe(alpha, x1, cmb, e_g[l], e_u[l], e_d[l], row(ln2_g, l), row(ln2_b, l))
    return x2.reshape(B, S, D)
```

```python
import functools

import jax
import jax.numpy as jnp
from jax import lax
from jax.experimental import pallas as pl
from jax.experimental.pallas import tpu as pltpu

F32 = jnp.float32
BF16 = jnp.bfloat16

LANES = 128
VMEM_LIMIT = 56 * 1024 * 1024

MLA_HEADS = 8
MLA_NOPE = 64
MLA_ROPE = 32
MLA_V = 64
MLA_Q_RANK = 256
MLA_KV_RANK = 128
ROPE_BASE = 10000.0
MLSTM_HEADS = 4
MLSTM_DH = 64
SSD_HEADS = 4
SSD_HEADDIM = 64
SSD_GROUPS = 2
SSD_STATE = 128
SSD_CONV = 5
N_GROUPS = 4
EXPERTS_PER_GROUP = 4
N_EXPERTS = 16
M_INIT = -1e30
LOG2E = 1.4426950408889634

MLA_OUT = MLA_HEADS * MLA_V
MLSTM_W = MLSTM_HEADS * MLSTM_DH
SSD_INNER = SSD_HEADS * SSD_HEADDIM
SSD_CONV_DIM = SSD_INNER + 2 * SSD_GROUPS * SSD_STATE
CHUNK = 128

P_MLA_W = 512
P_SSD_W = SSD_INNER + SSD_CONV_DIM + LANES


def _cparams(*sem):
    return pltpu.CompilerParams(dimension_semantics=sem, vmem_limit_bytes=VMEM_LIMIT)


def _dot(a, b):
    return jnp.dot(a, b, preferred_element_type=F32)


def _dot_nt(a, b):
    return lax.dot_general(a, b, (((1,), (1,)), ((), ())), preferred_element_type=F32)


def _dot_tn(a, b):
    return lax.dot_general(a, b, (((0,), (0,)), ((), ())), preferred_element_type=F32)


def _split3(x):
    hi = x.astype(BF16)
    r = x - hi.astype(F32)
    mid = r.astype(BF16)
    lo = (r - mid.astype(F32)).astype(BF16)
    return hi, mid, lo


def _cumsum_rows(tri_b16, x):
    hi, mid, lo = _split3(x)
    c = _dot(tri_b16, jnp.concatenate([hi, mid, lo], axis=1))
    return c[:, 2 * LANES:3 * LANES] + c[:, LANES:2 * LANES] + c[:, 0:LANES]


def _softplus(x):
    return jnp.maximum(x, 0.0) + jnp.log1p(jnp.exp(-jnp.abs(x)))


def _sigmoid(x):
    return 1.0 / (1.0 + jnp.exp(-x))


def _silu(x):
    return x * _sigmoid(x)


def _inproj_kernel(x_ref, wa_ref, wq_ref, wg_ref, wc_ref, pa_ref, pq_ref, pg_ref, pc_ref):
    x = x_ref[...].astype(BF16)
    pa_ref[...] = _dot(x, wa_ref[...])
    pq_ref[...] = _dot(x, wq_ref[...]).astype(pq_ref.dtype)
    pg_ref[...] = _dot(x, wg_ref[...])
    pc_ref[...] = _dot(x, wc_ref[...])


def _inproj(x2, wa, wq, wg, wc, tm=512):
    T, D = x2.shape
    row = lambda w: pl.BlockSpec((tm, w), lambda i: (i, 0))
    full = lambda a: pl.BlockSpec(a.shape, lambda i: (0, 0), pipeline_mode=pl.Buffered(1))
    return pl.pallas_call(
        _inproj_kernel,
        grid=(T // tm,),
        in_specs=[row(D), full(wa), full(wq), full(wg), full(wc)],
        out_specs=[row(P_MLA_W), row(3 * MLSTM_W), row(MLSTM_W + LANES), row(P_SSD_W)],
        out_shape=[jax.ShapeDtypeStruct((T, P_MLA_W), F32),
                   jax.ShapeDtypeStruct((T, 3 * MLSTM_W), BF16),
                   jax.ShapeDtypeStruct((T, MLSTM_W + LANES), F32),
                   jax.ShapeDtypeStruct((T, P_SSD_W), F32)],
        compiler_params=_cparams("parallel"),
        name="inproj",
    )(x2, wa, wq, wg, wc)


def _mla_prep_kernel(p_ref, cos_ref, sin_ref, qn_ref, kvn_ref, wq_ref, wk_ref, wv_ref,
                     q_ref, k_ref, v_ref):
    p = p_ref[...]
    cq = p[:, 0:MLA_Q_RANK]
    ckv = p[:, MLA_Q_RANK:MLA_Q_RANK + MLA_KV_RANK]
    krb = p[:, MLA_Q_RANK + MLA_KV_RANK:P_MLA_W]
    cqn = (cq * lax.rsqrt(jnp.mean(cq * cq, -1, keepdims=True) + 1e-6) * qn_ref[...]).astype(BF16)
    ckn = (ckv * lax.rsqrt(jnp.mean(ckv * ckv, -1, keepdims=True) + 1e-6) * kvn_ref[...]).astype(BF16)
    HW = MLA_HEADS * LANES
    q2 = _dot(cqn, wq_ref[...])
    k = _dot(ckn, wk_ref[...])
    v = _dot(ckn, wv_ref[...])
    vlane = lax.broadcasted_iota(jnp.int32, v.shape, 1)
    v_ref[...] = jnp.where(vlane % (2 * LANES) >= LANES, 1.0, v).astype(v_ref.dtype)
    cos = cos_ref[...]
    sin = sin_ref[...]
    lane = lax.broadcasted_iota(jnp.int32, cos.shape, 1)
    first_half = lane < MLA_NOPE + MLA_ROPE // 2

    def rope(t):
        partner = jnp.where(first_half, pltpu.roll(t, LANES - MLA_ROPE // 2, 1),
                            pltpu.roll(t, MLA_ROPE // 2, 1))
        return t * cos + partner * sin

    scale = (MLA_NOPE + MLA_ROPE) ** -0.5 * LOG2E
    cos_q = cos * scale
    sin_q = sin * scale
    kr = rope(krb)
    for h in range(MLA_HEADS):
        sl = slice(h * LANES, (h + 1) * LANES)
        q_ref[:, sl] = (q2[:, sl] * cos_q + q2[:, HW + h * LANES:HW + (h + 1) * LANES] * sin_q).astype(q_ref.dtype)
        k_ref[:, sl] = (k[:, sl] + kr).astype(k_ref.dtype)


def _mla_prep(p_mla, cos, sin, qn, kvn, wq, wk, wv, tm=1024):
    T = p_mla.shape[0]
    row = lambda w: pl.BlockSpec((tm, w), lambda i: (i, 0))
    full = lambda a: pl.BlockSpec(a.shape, lambda i: (0, 0), pipeline_mode=pl.Buffered(1))
    HW = MLA_HEADS * LANES
    return pl.pallas_call(
        _mla_prep_kernel,
        grid=(T // tm,),
        in_specs=[row(P_MLA_W), row(LANES), row(LANES), full(qn), full(kvn), full(wq), full(wk), full(wv)],
        out_specs=[row(HW), row(HW), row(HW)],
        out_shape=[jax.ShapeDtypeStruct((T, HW), BF16)] * 3,
        compiler_params=_cparams("parallel"),
        name="mla_prep",
    )(p_mla, cos, sin, qn, kvn, wq, wk, wv)


ATTN_SUB = 128


def _attn_kernel(q_ref, k_ref, v_ref, o_ref):
    tq = q_ref.shape[1]
    lane = lax.broadcasted_iota(jnp.int32, (ATTN_SUB, LANES), 1)
    for r in range(tq // ATTN_SUB):
        rows = slice(r * ATTN_SUB, (r + 1) * ATTN_SUB)
        outs = []
        for hh in range(2):
            sl = slice(hh * LANES, (hh + 1) * LANES)
            s = _dot_nt(q_ref[0, rows, sl], k_ref[0, :, sl])
            p = jnp.exp2(s - jnp.max(s, -1, keepdims=True)).astype(BF16)
            ov = _dot(p, v_ref[0])
            outs.append(ov[:, 0:LANES] / ov[:, LANES:])
        o_ref[0, rows, :] = jnp.where(lane < MLA_V, outs[0], outs[1]).astype(o_ref.dtype)


def _attention(q, k, v):
    B, S, _ = q.shape
    tq = S
    return pl.pallas_call(
        _attn_kernel,
        grid=(B, MLA_HEADS // 2, S // tq),
        in_specs=[pl.BlockSpec((1, tq, 2 * LANES), lambda b, j, i: (b, i, j)),
                  pl.BlockSpec((1, S, 2 * LANES), lambda b, j, i: (b, 0, j)),
                  pl.BlockSpec((1, S, 2 * LANES), lambda b, j, i: (b, 0, j))],
        out_specs=pl.BlockSpec((1, tq, LANES), lambda b, j, i: (b, i, j)),
        out_shape=jax.ShapeDtypeStruct((B, S, MLA_OUT), BF16),
        compiler_params=_cparams("parallel", "parallel", "arbitrary"),
        name="mla_attn",
    )(q, k, v)


_ML_FCOL = lambda d, h: 8 * d + MLSTM_HEADS + h
_ML_ICOL = lambda d, h: 8 * d + h


def _scan_max_rows(x, row, bwd_lanes):
    L = x.shape[0]
    neg = -jnp.inf
    xf, xb = x, x
    k = 1
    while k < L:
        xf = jnp.maximum(xf, jnp.where(row >= k, pltpu.roll(xf, k, 0), neg))
        xb = jnp.maximum(xb, jnp.where(row < L - k, pltpu.roll(xb, L - k, 0), neg))
        k *= 2
    return jnp.where(bwd_lanes, xb, xf)


def _mlstm_kernel(qkv_ref, og_ref, gb_ref, ng_ref, avg_ref, y_ref,
                  u_s, b_s, tot_s, mdec_s, mpf_s, mpb_s, ds_s, sp_s, st_s):
    S = qkv_ref.shape[1]
    L = CHUNK
    nc = S // L
    H = MLSTM_HEADS
    SW = 2 * LANES
    ti = lax.broadcasted_iota(jnp.int32, (L, L), 0)
    si = lax.broadcasted_iota(jnp.int32, (L, L), 1)
    tri = (si <= ti, si >= ti)
    tri_b16 = tri[0].astype(BF16)
    lane = lax.broadcasted_iota(jnp.int32, (L, LANES), 1)
    row = lax.broadcasted_iota(jnp.int32, (L, LANES), 0)
    lane1 = lax.broadcasted_iota(jnp.int32, (1, LANES), 1)
    bwd_lanes = lane >= 2 * H
    bwd_lanes1 = lane1 >= 2 * H
    lo = lane < MLSTM_DH
    gbias = gb_ref[...]
    ones_b16 = jnp.ones((L, LANES), BF16)
    zero_b16 = jnp.zeros((L, LANES), BF16)
    QO, KO, VO = 0, MLSTM_W, 2 * MLSTM_W
    OO, GO = 0, MLSTM_W

    def head_slices(rows, j, e, off):
        sel = lo if e == 0 else ~lo
        return jnp.where(sel, qkv_ref[0, rows, off + j * LANES:off + (j + 1) * LANES], zero_b16)

    def phase_a(c, carry):
        rows = pl.ds(pl.multiple_of(c * L, L), L)
        G = og_ref[0, rows, GO:GO + LANES] + gbias
        LF = -_softplus(-G)
        pre = _cumsum_rows(tri_b16, LF)
        tot = pre[L - 1:L, :]
        Bm = jnp.where(bwd_lanes, tot - pre + LF, pre)
        U = pltpu.roll(G, H, 1) - Bm
        u_s[rows, :] = U
        b_s[rows, :] = Bm
        dec = tot + U
        mdec = jnp.max(dec, 0, keepdims=True)
        W = jnp.exp(dec - mdec)
        tot_s[c] = tot
        mdec_s[c] = mdec
        for j in range(H // 2):
            for e in range(2):
                h = 2 * j + e
                km = head_slices(rows, j, e, KO)
                vm = head_slices(rows, j, e, VO).astype(F32)
                parts = []
                for d in range(2):
                    wb = jnp.broadcast_to(W[:, _ML_FCOL(d, h):_ML_FCOL(d, h) + 1], (L, LANES))
                    parts += [wb * vm, wb]
                wv = jnp.concatenate(parts, axis=1).astype(BF16)
                ds_s[c * H + h] = _dot_tn(km, wv)
        return carry

    lax.fori_loop(0, nc, phase_a, 0, unroll=4)

    st_s[...] = jnp.zeros(st_s.shape, F32)

    def phase_b(i, m):
        cf, cb = i, nc - 1 - i
        totm = jnp.where(bwd_lanes1, tot_s[cb], tot_s[cf])
        mdecm = jnp.where(bwd_lanes1, mdec_s[cb], mdec_s[cf])
        mpf_s[cf] = m
        mpb_s[cb] = m
        m_new = jnp.maximum(totm + m, mdecm)
        w_c = jnp.exp(totm + m - m_new)
        w_d = jnp.exp(mdecm - m_new)
        for h in range(H):
            for d, c in ((0, cf), (1, cb)):
                part = slice(d * SW, (d + 1) * SW)
                col = slice(_ML_FCOL(d, h), _ML_FCOL(d, h) + 1)
                st = st_s[h, :, part]
                sp_s[c * H + h, :, part] = st.astype(BF16)
                st_s[h, :, part] = w_c[:, col] * st + w_d[:, col] * ds_s[c * H + h, :, part]
        return m_new

    lax.fori_loop(0, nc, phase_b, jnp.full((1, LANES), M_INIT, F32), unroll=4)

    ng = ng_ref[...]
    avg = avg_ref[...]

    def head_mean(t):
        hi = t.astype(BF16)
        lo_part = (t - hi.astype(F32)).astype(BF16)
        return _dot(jnp.concatenate([hi, lo_part], axis=1), avg)

    def phase_c(c, carry):
        rows = pl.ds(pl.multiple_of(c * L, L), L)
        U = u_s[rows, :]
        Bm = b_s[rows, :]
        UT = U.T
        m_prev = jnp.where(bwd_lanes1, mpb_s[c], mpf_s[c])
        Gm = Bm + m_prev
        Mt = jnp.maximum(Gm, Bm + _scan_max_rows(U, row, bwd_lanes))
        Z = Bm - Mt
        WI = jnp.exp(Gm - Mt)
        FL = jnp.exp(-Mt)
        for j in range(H // 2):
            pair = slice(j * LANES, (j + 1) * LANES)
            kb = qkv_ref[0, rows, KO + j * LANES:KO + (j + 1) * LANES]
            hsum = []
            for e in range(2):
                h = 2 * j + e
                qm = head_slices(rows, j, e, QO)
                vaug = jnp.concatenate([head_slices(rows, j, e, VO), ones_b16], axis=1)
                qk = _dot_nt(qm, kb)
                inter = _dot(qm, sp_s[c * H + h])
                ps = []
                for d in range(2):
                    fc = _ML_FCOL(d, h)
                    e_ts = jnp.where(tri[d], UT[fc:fc + 1, :], -jnp.inf) + Z[:, fc:fc + 1]
                    ps.append((qk * jnp.exp(e_ts)).astype(BF16))
                pv = _dot(jnp.concatenate(ps, axis=0), vaug)
                hs = None
                for d in range(2):
                    fc = _ML_FCOL(d, h)
                    nd = pv[d * L:(d + 1) * L] + WI[:, fc:fc + 1] * inter[:, d * SW:(d + 1) * SW]
                    hd = nd[:, 0:LANES] / jnp.maximum(jnp.abs(nd[:, LANES:SW]), FL[:, fc:fc + 1])
                    hs = hd if hs is None else hs + hd
                hsum.append(hs)
            hp = jnp.where(lo, hsum[0], hsum[1])
            dlt = hp - head_mean(hp)
            var = head_mean(dlt * dlt)
            hn = dlt * lax.rsqrt(var + 1e-5) * ng[:, pair]
            o = og_ref[0, rows, OO + j * LANES:OO + (j + 1) * LANES]
            y_ref[0, rows, pair] = (_sigmoid(o) * hn).astype(y_ref.dtype)
        return carry

    lax.fori_loop(0, nc, phase_c, 0, unroll=4)


def _mlstm(p_qkv, p_og, gate_bias, norm_g, avg):
    B, S, _ = p_qkv.shape
    nc = S // CHUNK
    H = MLSTM_HEADS
    vec = lambda: pltpu.VMEM((nc, 1, LANES), F32)
    return pl.pallas_call(
        _mlstm_kernel,
        grid=(B,),
        in_specs=[pl.BlockSpec((1, S, 3 * MLSTM_W), lambda b: (b, 0, 0)),
                  pl.BlockSpec((1, S, MLSTM_W + LANES), lambda b: (b, 0, 0)),
                  pl.BlockSpec((1, LANES), lambda b: (0, 0)),
                  pl.BlockSpec((1, MLSTM_W), lambda b: (0, 0)),
                  pl.BlockSpec(avg.shape, lambda b: (0, 0))],
        out_specs=pl.BlockSpec((1, S, MLSTM_W), lambda b: (b, 0, 0)),
        out_shape=jax.ShapeDtypeStruct((B, S, MLSTM_W), BF16),
        scratch_shapes=[pltpu.VMEM((S, LANES), F32), pltpu.VMEM((S, LANES), F32),
                        vec(), vec(), vec(), vec(),
                        pltpu.VMEM((nc * H, LANES, 4 * LANES), F32),
                        pltpu.VMEM((nc * H, LANES, 4 * LANES), BF16),
                        pltpu.VMEM((H, LANES, 4 * LANES), F32)],
        compiler_params=_cparams("parallel"),
        name="mlstm",
    )(p_qkv, p_og, gate_bias, norm_g, avg)


def _ssd_kernel(p_ref, cw_ref, cb_ref, dtb_ref, alog_ref, dsk_ref, ng_ref, y_ref,
                xpad_ref, xc_ref, cs_s, dt_s, tot_s, ds_s, sp_s, st_ref):
    S = p_ref.shape[1]
    L = CHUNK
    nc = S // L
    PAD = 8
    ZO, XO, DO = 0, SSD_INNER, SSD_INNER + SSD_CONV_DIM

    zpad = jnp.zeros((PAD, SSD_CONV_DIM), F32)
    xpad_ref[0:PAD, :] = zpad
    xpad_ref[PAD + S:PAD + S + PAD, :] = zpad
    R = 256
    for r in range(S // R):
        xpad_ref[PAD + r * R:PAD + (r + 1) * R, :] = p_ref[0, r * R:(r + 1) * R, XO:XO + SSD_CONV_DIM]
    half = SSD_CONV // 2
    for r in range(S // R):
        win = xpad_ref[r * R:r * R + R + 2 * PAD, :]
        acc = jnp.zeros((R, SSD_CONV_DIM), F32) + cb_ref[...]
        for kk in range(SSD_CONV):
            sh = (half - kk) % (R + 2 * PAD)
            tap = win if sh == 0 else pltpu.roll(win, sh, 0)
            acc = acc + tap[PAD:PAD + R, :] * cw_ref[kk:kk + 1, :]
        xc_ref[r * R:(r + 1) * R, :] = _silu(acc)

    ti = lax.broadcasted_iota(jnp.int32, (L, L), 0)
    si = lax.broadcasted_iota(jnp.int32, (L, L), 1)
    tri = (si <= ti, si >= ti)
    tri_b16 = tri[0].astype(BF16)
    lane = lax.broadcasted_iota(jnp.int32, (L, LANES), 1)
    lane1 = lax.broadcasted_iota(jnp.int32, (1, LANES), 1)
    lo = lane < SSD_HEADDIM
    lo1 = lane1 < SSD_HEADDIM
    bwd_lanes = lane >= SSD_HEADS
    a_neg = -jnp.exp(alog_ref[...])
    dtb = dtb_ref[...]
    BO, CO = SSD_INNER, SSD_INNER + SSD_GROUPS * SSD_STATE
    NG = SSD_GROUPS
    col_of = lambda d, g, e: d * SSD_HEADS + NG * g + e

    def pair_cols(a, d, g, mask):
        ce = col_of(d, g, 0)
        return jnp.where(mask, a[:, ce:ce + 1], a[:, ce + 1:ce + 2])

    def phase_a(c, carry):
        rows = pl.ds(pl.multiple_of(c * L, L), L)
        dt = _softplus(p_ref[0, rows, DO:DO + LANES] + dtb)
        dta = dt * a_neg
        pre = _cumsum_rows(tri_b16, dta)
        tot = pre[L - 1:L, :]
        cs = jnp.where(bwd_lanes, tot - pre + dta, pre)
        cs_s[rows, :] = cs
        dt_s[rows, :] = dt
        dec = jnp.exp(tot - cs) * dt
        for g in range(NG):
            xp = xc_ref[rows, g * LANES:(g + 1) * LANES]
            Bg = xc_ref[rows, BO + g * LANES:BO + (g + 1) * LANES].astype(BF16)
            xdt = jnp.concatenate([xp * pair_cols(dec, d, g, lo) for d in range(2)], axis=1)
            inc = _dot_tn(Bg, xdt.astype(BF16))
            for d in range(2):
                tot_s[c * 2 * NG + d * NG + g] = pair_cols(tot, d, g, lo1)
                ds_s[c * 2 * NG + d * NG + g] = inc[:, d * LANES:(d + 1) * LANES]
        return carry

    lax.fori_loop(0, nc, phase_a, 0, unroll=8)

    st_ref[...] = jnp.zeros(st_ref.shape, F32)

    def phase_b(i, carry):
        for d in range(2):
            c = i if d == 0 else nc - 1 - i
            for g in range(NG):
                k = d * NG + g
                st = st_ref[k]
                sp_s[c * NG + g, :, d * LANES:(d + 1) * LANES] = st.astype(BF16)
                st_ref[k] = jnp.exp(tot_s[c * 2 * NG + k]) * st + ds_s[c * 2 * NG + k]
        return carry

    lax.fori_loop(0, nc, phase_b, 0)

    dsk = dsk_ref[...]
    ng = ng_ref[...]

    def phase_c(c, carry):
        rows = pl.ds(pl.multiple_of(c * L, L), L)
        cs = cs_s[rows, :]
        cst = cs.T
        dtt = dt_s[rows, :].T
        for g in range(NG):
            pair = slice(g * LANES, (g + 1) * LANES)
            xp = xc_ref[rows, pair]
            xb = xp.astype(BF16)
            Bg = xc_ref[rows, BO + g * LANES:BO + (g + 1) * LANES].astype(BF16)
            Cg = xc_ref[rows, CO + g * LANES:CO + (g + 1) * LANES].astype(BF16)
            CB = _dot_nt(Cg, Bg)
            yi = _dot(Cg, sp_s[c * NG + g])
            bc = [[jnp.broadcast_to(cs[:, col_of(d, g, e):col_of(d, g, e) + 1], (L, LANES))
                   for e in range(2)] for d in range(2)]
            ys = []
            for e in range(2):
                W = None
                for d in range(2):
                    col = col_of(d, g, e)
                    arg = jnp.where(tri[d], bc[d][e] - cst[col:col + 1, :], -jnp.inf)
                    w = jnp.exp(arg) * dtt[col:col + 1, :]
                    W = w if W is None else W + w
                ys.append(_dot((CB * W).astype(BF16), xb))
            y = jnp.where(lo, ys[0], ys[1])
            for d in range(2):
                y = y + yi[:, d * LANES:(d + 1) * LANES] * jnp.exp(jnp.where(lo, bc[d][0], bc[d][1]))
            y = y + xp * dsk[:, pair]
            y = y * _silu(p_ref[0, rows, ZO + g * LANES:ZO + (g + 1) * LANES])
            y = y * lax.rsqrt(jnp.mean(y * y, -1, keepdims=True) + 1e-6) * ng[:, pair]
            y_ref[0, rows, pair] = y.astype(y_ref.dtype)
        return carry

    lax.fori_loop(0, nc, phase_c, 0, unroll=4)


def _ssd(p_ssd, conv_w, conv_b, dt_bias, a_log, d_skip, norm_g):
    B, S, _ = p_ssd.shape
    nc = S // CHUNK
    NG = SSD_GROUPS
    full = lambda a: pl.BlockSpec(a.shape, lambda b: (0, 0))
    return pl.pallas_call(
        _ssd_kernel,
        grid=(B,),
        in_specs=[pl.BlockSpec((1, S, P_SSD_W), lambda b: (b, 0, 0)),
                  full(conv_w), full(conv_b), full(dt_bias), full(a_log), full(d_skip), full(norm_g)],
        out_specs=pl.BlockSpec((1, S, SSD_INNER), lambda b: (b, 0, 0)),
        out_shape=jax.ShapeDtypeStruct((B, S, SSD_INNER), BF16),
        scratch_shapes=[pltpu.VMEM((S + 16, SSD_CONV_DIM), F32), pltpu.VMEM((S, SSD_CONV_DIM), F32),
                        pltpu.VMEM((S, LANES), F32), pltpu.VMEM((S, LANES), F32),
                        pltpu.VMEM((nc * 2 * NG, 1, LANES), F32),
                        pltpu.VMEM((nc * 2 * NG, SSD_STATE, LANES), F32),
                        pltpu.VMEM((nc * NG, SSD_STATE, 2 * LANES), BF16),
                        pltpu.VMEM((2 * NG, SSD_STATE, LANES), F32)],
        compiler_params=_cparams("parallel"),
        name="ssd",
    )(p_ssd, conv_w, conv_b, dt_bias, a_log, d_skip, norm_g)


def _layernorm(u, g, b):
    mu = jnp.mean(u, -1, keepdims=True)
    d = u - mu
    var = jnp.mean(d * d, -1, keepdims=True)
    return d * lax.rsqrt(var + 1e-5) * g + b


def _route(logits):
    lane_i = lax.broadcasted_iota(jnp.int32, logits.shape, 1)
    lane = lane_i.astype(F32)
    group_of_lane = jnp.right_shift(lane_i - N_GROUPS, 2).astype(F32)
    neg = -jnp.inf
    big = 1e6
    glm = jnp.where(lane_i < N_GROUPS, logits, neg)
    gmax = jnp.max(glm, -1, keepdims=True)
    gp = 1.0 / jnp.sum(jnp.exp(glm - gmax), -1, keepdims=True)
    gi = jnp.min(jnp.where(glm == gmax, lane, big), -1, keepdims=True)
    in_group = (lane_i >= N_GROUPS) & (lane_i < N_GROUPS + N_EXPERTS) & (group_of_lane == gi)
    elm = jnp.where(in_group, logits, neg)
    e1 = jnp.max(elm, -1, keepdims=True)
    i1 = jnp.min(jnp.where(elm == e1, lane, big), -1, keepdims=True)
    elm2 = jnp.where(lane == i1, neg, elm)
    e2 = jnp.max(elm2, -1, keepdims=True)
    i2 = jnp.min(jnp.where(elm2 == e2, lane, big), -1, keepdims=True)
    r = jnp.exp(e2 - e1)
    p1 = 1.0 / (1.0 + r)
    p2 = r / (1.0 + r)
    return jnp.where(lane == i1, gp * p1, jnp.where(lane == i2, gp * p2, 0.0))


OUT_SUB = 256


def _outproj_kernel(alpha, x_ref, ya_ref, yb_ref, yc_ref, wo_ref, g_ref, b_ref, wr_ref, br_ref,
                    x1_ref, cmb_ref, mix0_ref, mix1_ref):
    i = pl.program_id(0)
    tm = x_ref.shape[0]

    @pl.when(i == 0)
    def _():
        mix1_ref[...] = jnp.zeros(mix1_ref.shape, F32)

    def body(mix_w, mix_r):
        for r in range(tm // OUT_SUB):
            rows = slice(r * OUT_SUB, (r + 1) * OUT_SUB)
            mix_w[rows, :] = (_dot(ya_ref[rows, :], wo_ref[0:MLA_OUT, :])
                              + _dot(yb_ref[rows, :], wo_ref[MLA_OUT:MLA_OUT + MLSTM_W, :])
                              + _dot(yc_ref[rows, :], wo_ref[MLA_OUT + MLSTM_W:, :]))
            x1 = _layernorm(alpha * x_ref[rows, :] + mix_r[rows, :], g_ref[...], b_ref[...])
            x1_ref[rows, :] = x1
            x_hi = x1.astype(BF16)
            x_lo = (x1 - x_hi.astype(F32)).astype(BF16)
            a = _dot(x_hi, wr_ref[...])
            c = _dot(x_lo, wr_ref[...])
            logits = (a[:, 0:LANES] + (a[:, LANES:] + c[:, 0:LANES]) + c[:, LANES:]) + br_ref[...]
            cmb_ref[rows, :] = _route(logits)

    @pl.when(i % 2 == 0)
    def _():
        body(mix0_ref, mix1_ref)

    @pl.when(i % 2 == 1)
    def _():
        body(mix1_ref, mix0_ref)


def _outproj(alpha, x2, ya, yb, yc, wo, g, b, wr, br, tm=1024):
    T, D = x2.shape
    n = T // tm
    cur = lambda w: pl.BlockSpec((tm, w), lambda i: (jnp.minimum(i, n - 1), 0))
    prev = lambda w: pl.BlockSpec((tm, w), lambda i: (jnp.maximum(i - 1, 0), 0))
    full = lambda a: pl.BlockSpec(a.shape, lambda i: (0, 0), pipeline_mode=pl.Buffered(1))
    return pl.pallas_call(
        functools.partial(_outproj_kernel, alpha),
        grid=(n + 1,),
        in_specs=[prev(D), cur(MLA_OUT), cur(MLSTM_W), cur(SSD_INNER), full(wo), full(g), full(b),
                  full(wr), full(br)],
        out_specs=[prev(D), prev(LANES)],
        out_shape=[jax.ShapeDtypeStruct((T, D), F32), jax.ShapeDtypeStruct((T, LANES), F32)],
        scratch_shapes=[pltpu.VMEM((tm, D), F32), pltpu.VMEM((tm, D), F32)],
        compiler_params=_cparams("arbitrary"),
        name="outproj_ln_router",
    )(x2, ya, yb, yc, wo, g, b, wr, br)


MOE_SUB = 512


def _moe_kernel(alpha, x_ref, cmb_ref, wg_ref, wu_ref, wd_ref, g_ref, b_ref, o_ref, xb_ref, acc_ref):
    grp = pl.program_id(1)

    @pl.when(grp == 0)
    def _():
        xb_ref[...] = x_ref[...].astype(BF16)
        acc_ref[...] = jnp.zeros(acc_ref.shape, F32)

    lane = lax.broadcasted_iota(jnp.int32, (MOE_SUB, LANES), 1)
    for r in range(x_ref.shape[0] // MOE_SUB):
        rows = slice(r * MOE_SUB, (r + 1) * MOE_SUB)
        xb = xb_ref[rows, :]
        cmb = cmb_ref[rows, :]
        hs = []
        for e in range(EXPERTS_PER_GROUP):
            col = N_GROUPS + grp * EXPERTS_PER_GROUP + e
            c = jnp.sum(jnp.where(lane == col, cmb, 0.0), -1, keepdims=True)
            hs.append((_silu(_dot(xb, wg_ref[e])) * _dot(xb, wu_ref[e]) * c).astype(BF16))
        acc_ref[rows, :] += _dot(jnp.concatenate(hs, axis=1), wd_ref[0])

    @pl.when(grp == N_GROUPS - 1)
    def _():
        o_ref[...] = _layernorm(alpha * x_ref[...] + acc_ref[...], g_ref[...], b_ref[...])


def _moe(alpha, x1, cmb, wg, wu, wd, g, b, tm=1024):
    T, D = x1.shape
    F = wg.shape[-1]
    E = EXPERTS_PER_GROUP
    return pl.pallas_call(
        functools.partial(_moe_kernel, alpha),
        grid=(T // tm, N_GROUPS),
        in_specs=[pl.BlockSpec((tm, D), lambda i, e: (i, 0)),
                  pl.BlockSpec((tm, LANES), lambda i, e: (i, 0)),
                  pl.BlockSpec((E, D, F), lambda i, e: (e, 0, 0)),
                  pl.BlockSpec((E, D, F), lambda i, e: (e, 0, 0)),
                  pl.BlockSpec((1, E * F, D), lambda i, e: (e, 0, 0)),
                  pl.BlockSpec((1, D), lambda i, e: (0, 0)),
                  pl.BlockSpec((1, D), lambda i, e: (0, 0))],
        out_specs=pl.BlockSpec((tm, D), lambda i, e: (i, 0)),
        out_shape=jax.ShapeDtypeStruct((T, D), F32),
        scratch_shapes=[pltpu.VMEM((tm, D), BF16), pltpu.VMEM((tm, D), F32)],
        compiler_params=_cparams("parallel", "arbitrary"),
        name="moe_ln",
    )(x1, cmb, wg, wu, wd, g, b)


def _pad_lanes(a, width=LANES):
    return jnp.pad(a, [(0, 0)] * (a.ndim - 1) + [(0, width - a.shape[-1])])


def kernel(x, positions, w_in, mla_q_norm, mla_kv_norm, mla_w_uq, mla_w_ukv, mlstm_gate_bias, mlstm_norm,
           ssd_conv_w, ssd_conv_b, ssd_dt_bias, ssd_a_log, ssd_d, ssd_norm, w_out, ln1_g, ln1_b,
           router_group_w, router_group_b, router_expert_w, router_expert_b,
           expert_w_gate, expert_w_up, expert_w_down, ln2_g, ln2_b):
    B, S, D = x.shape
    depth = w_in.shape[0]
    T = B * S
    alpha = (2 * depth) ** 0.25

    inv = ROPE_BASE ** (-jnp.arange(0, MLA_ROPE, 2, dtype=F32) / MLA_ROPE)
    ang = positions.astype(F32).reshape(T, 1) * inv
    cos, sin = jnp.cos(ang), jnp.sin(ang)
    ones = jnp.ones((T, MLA_NOPE), F32)
    zeros = jnp.zeros((T, MLA_NOPE), F32)
    pad = jnp.zeros((T, LANES - MLA_NOPE - MLA_ROPE), F32)
    cos_t = jnp.concatenate([ones, cos, cos, pad], -1)
    sin_t = jnp.concatenate([zeros, -sin, sin, pad], -1)

    o = [0]
    for s in (MLA_Q_RANK, MLA_KV_RANK, MLA_ROPE, MLSTM_W, MLSTM_W, MLSTM_W, MLSTM_W, 4 * MLSTM_HEADS,
              SSD_INNER, SSD_CONV_DIM, 2 * SSD_HEADS):
        o.append(o[-1] + s)
    w_b = w_in.astype(BF16)
    zc = lambda n: jnp.zeros((depth, D, n), BF16)
    w_a = jnp.concatenate([w_b[..., o[0]:o[2]], zc(MLA_NOPE), w_b[..., o[2]:o[3]],
                           zc(LANES - MLA_NOPE - MLA_ROPE)], -1)
    w_qkv = jnp.concatenate([w_b[..., o[3]:o[4]], w_b[..., o[4]:o[5]] * (MLSTM_DH ** -0.5),
                             w_b[..., o[5]:o[6]]], -1)
    w_og = jnp.concatenate([w_b[..., o[6]:o[8]], zc(LANES - 4 * MLSTM_HEADS)], -1)
    w_c = jnp.concatenate([w_b[..., o[8]:o[11]], zc(LANES - 2 * SSD_HEADS)], -1)
    head_of_lane = jnp.arange(LANES) // MLSTM_DH
    avg_blk = (head_of_lane[:, None] == head_of_lane[None, :]).astype(BF16) * (1.0 / MLSTM_DH)
    ml_avg = jnp.concatenate([avg_blk, avg_blk], 0)

    wq4 = mla_w_uq.reshape(depth, MLA_Q_RANK, MLA_HEADS, MLA_NOPE + MLA_ROPE)
    half = MLA_ROPE // 2
    wq_swap = jnp.concatenate([jnp.zeros_like(wq4[..., :MLA_NOPE]), wq4[..., MLA_NOPE + half:],
                               wq4[..., MLA_NOPE:MLA_NOPE + half]], -1)
    flat = lambda w: _pad_lanes(w).reshape(depth, MLA_Q_RANK, MLA_HEADS * LANES)
    wq = jnp.concatenate([flat(wq4), flat(wq_swap)], -1).astype(BF16)
    wkv = mla_w_ukv.reshape(depth, MLA_KV_RANK, MLA_HEADS, MLA_NOPE + MLA_V)
    wk = _pad_lanes(wkv[..., :MLA_NOPE]).reshape(depth, MLA_KV_RANK, MLA_HEADS * LANES).astype(BF16)
    wv = wkv[..., MLA_NOPE:].reshape(depth, MLA_KV_RANK, MLA_HEADS // 2, 2 * MLA_V)
    wv = _pad_lanes(wv, 2 * LANES).reshape(depth, MLA_KV_RANK, MLA_HEADS * LANES).astype(BF16)

    gate_bias = _pad_lanes(mlstm_gate_bias.reshape(depth, 1, 4 * MLSTM_HEADS))
    dt_bias = _pad_lanes(ssd_dt_bias.reshape(depth, 1, 2 * SSD_HEADS))
    a_log = _pad_lanes(ssd_a_log.reshape(depth, 1, 2 * SSD_HEADS))
    d_skip = jnp.repeat(ssd_d, SSD_HEADDIM, axis=-1).reshape(depth, 1, SSD_INNER)
    conv_w = jnp.pad(ssd_conv_w, ((0, 0), (0, 8 - SSD_CONV), (0, 0)))

    w_o = w_out.astype(BF16)
    w_r = _pad_lanes(jnp.concatenate(
        [router_group_w, router_expert_w.transpose(0, 2, 1, 3).reshape(depth, D, N_EXPERTS)], -1))
    w_r_hi = w_r.astype(BF16)
    w_r2 = jnp.concatenate([w_r_hi, (w_r - w_r_hi.astype(F32)).astype(BF16)], -1)
    b_r = _pad_lanes(jnp.concatenate(
        [router_group_b, router_expert_b.reshape(depth, N_EXPERTS)], -1).reshape(depth, 1, -1))
    e_g = expert_w_gate.astype(BF16)
    e_u = expert_w_up.astype(BF16)
    e_d = expert_w_down.reshape(depth, N_GROUPS, -1, D).astype(BF16)
    row = lambda a, l: a[l].reshape(1, -1)

    x2 = x.reshape(T, D)
    for l in range(depth):
        p_mla, p_qkv, p_og, p_ssd = _inproj(x2, w_a[l], w_qkv[l], w_og[l], w_c[l])
        q, k, v = _mla_prep(p_mla, cos_t, sin_t, row(mla_q_norm, l), row(mla_kv_norm, l), wq[l], wk[l], wv[l])
        y_a = _attention(q.reshape(B, S, -1), k.reshape(B, S, -1), v.reshape(B, S, -1))
        y_b = _mlstm(p_qkv.reshape(B, S, -1), p_og.reshape(B, S, -1), gate_bias[l], row(mlstm_norm, l), ml_avg)
        y_c = _ssd(p_ssd.reshape(B, S, -1), conv_w[l], row(ssd_conv_b, l), dt_bias[l], a_log[l],
                   d_skip[l], row(ssd_norm, l))
        x1, cmb = _outproj(alpha, x2, y_a.reshape(T, -1), y_b.reshape(T, -1), y_c.reshape(T, -1),
                           w_o[l], row(ln1_g, l), row(ln1_b, l), w_r2[l], b_r[l])
        x2 = _moe(alpha, x1, cmb, e_g[l], e_u[l], e_d[l], row(ln2_g, l), row(ln2_b, l))
    return x2.reshape(B, S, D)
```
